```python
import jax
import jax.numpy as jnp
from jax import lax
import numpy as np

D_MODEL = 1024
BATCH = 32
SEQ = 2048
DEPTH = 2

N_META = 16
M_HEADS = 4
M_DQK = 128
M_DV = D_MODEL // M_HEADS
CHUNK = 64
PAD_FRONT = CHUNK - N_META
CONV_WIDTH = D_MODEL
CONV_K = 3
N_GROUPS = 4
EXP_PER_GROUP = 4
N_EXPERTS = N_GROUPS * EXP_PER_GROUP
TOP_K = 2
D_EXPERT = 256
EPS = 1e-6
F_BIAS = 3.0
QK_W = M_HEADS * M_DQK
V_W = M_HEADS * M_DV
IN_WIDTHS = (QK_W, QK_W, V_W, M_HEADS, M_HEADS, V_W, CONV_WIDTH, CONV_WIDTH, CONV_WIDTH, CONV_WIDTH)
IN_TOTAL = sum(IN_WIDTHS)

kernel_name = 'hybrid_mlstm_shortconv_hmoe'


def rmsnorm(x, g):
    xf = x.astype(jnp.float32)
    y = xf * lax.rsqrt(jnp.mean(xf * xf, axis=-1, keepdims=True) + EPS)
    return (y * g.astype(jnp.float32)).astype(x.dtype)


def mlstm_chunkwise(q, k, v, ig, lf):
    Bsz, H, T, Dqk = q.shape
    Dv = v.shape[-1]
    nc = T // CHUNK

    def to_chunks(a):
        a = a.reshape((Bsz, H, nc, CHUNK) + a.shape[3:])
        return jnp.moveaxis(a, 2, 0)

    causal = jnp.tril(jnp.ones((CHUNK, CHUNK), dtype=bool))

    def step(carry, xs):
        C, n, m = carry
        qc, kc, vc, ic, fc = xs
        b = jnp.cumsum(fc, axis=-1)
        log_d = b[..., :, None] - b[..., None, :] + ic[..., None, :]
        log_d = jnp.where(causal, log_d, -jnp.inf)
        inter = b + m[..., None]
        m_t = jnp.maximum(inter, jnp.max(log_d, axis=-1))
        dmat = jnp.exp(log_d - m_t[..., None])
        s = jnp.einsum('bhtd,bhsd->bhts', qc, kc) * dmat
        e_inter = jnp.exp(inter - m_t)
        num = jnp.einsum('bhts,bhsv->bhtv', s, vc) + e_inter[..., None] * jnp.einsum('bhtd,bhdv->bhtv', qc, C)
        den = jnp.sum(s, axis=-1) + e_inter * jnp.einsum('bhtd,bhd->bht', qc, n)
        h = num / jnp.maximum(jnp.abs(den), jnp.exp(-m_t))[..., None]
        bL = b[..., -1]
        log_w = bL[..., None] - b + ic
        m_new = jnp.maximum(bL + m, jnp.max(log_w, axis=-1))
        w = jnp.exp(log_w - m_new[..., None])
        decay = jnp.exp(bL + m - m_new)
        C = decay[..., None, None] * C + jnp.einsum('bhs,bhsd,bhsv->bhdv', w, kc, vc)
        n = decay[..., None] * n + jnp.einsum('bhs,bhsd->bhd', w, kc)
        return (C, n, m_new), h

    init = (jnp.zeros((Bsz, H, Dqk, Dv), jnp.float32),
            jnp.zeros((Bsz, H, Dqk), jnp.float32),
            jnp.zeros((Bsz, H), jnp.float32))
    _, hs = lax.scan(step, init, (to_chunks(q), to_chunks(k), to_chunks(v), to_chunks(ig), to_chunks(lf)))
    return jnp.moveaxis(hs, 0, 2).reshape(Bsz, H, T, Dv)


def hybrid_mixer(h, w_in, b_if, conv_w, mh_gain, w_out):
    Bsz, L, _ = h.shape
    f32 = jnp.float32
    split_at = [int(s) for s in np.cumsum(IN_WIDTHS)[:-1]]
    q, k, v, ig, fg, g_m, cx, cb, cc, g_c = jnp.split(h @ w_in, split_at, axis=-1)

    def heads(a, d):
        return a.reshape(Bsz, L, M_HEADS, d).transpose(0, 2, 1, 3).astype(f32)

    q = heads(q, M_DQK) * (M_DQK ** -0.5)
    k = heads(k, M_DQK)
    v = heads(v, M_DV)
    ig = (ig.astype(f32) + b_if[:M_HEADS].astype(f32)).transpose(0, 2, 1)
    lf = jax.nn.log_sigmoid(fg.astype(f32) + b_if[M_HEADS:].astype(f32)).transpose(0, 2, 1)
    pad4 = ((0, 0), (0, 0), (PAD_FRONT, 0), (0, 0))
    pad3 = ((0, 0), (0, 0), (PAD_FRONT, 0))
    hm = mlstm_chunkwise(jnp.pad(q, pad4), jnp.pad(k, pad4), jnp.pad(v, pad4),
                         jnp.pad(ig, pad3, constant_values=-jnp.inf),
                         jnp.pad(lf, pad3))[:, :, PAD_FRONT:]
    hm = hm * lax.rsqrt(jnp.mean(hm * hm, axis=-1, keepdims=True) + EPS)
    hm = hm.transpose(0, 2, 1, 3).reshape(Bsz, L, V_W) * mh_gain.astype(f32)
    y_m = jax.nn.sigmoid(g_m.astype(f32)) * hm

    u = cc * cx
    up = jnp.pad(u, ((0, 0), (CONV_K - 1, 0), (0, 0)))
    conv = sum(conv_w[j] * up[:, j:j + L] for j in range(CONV_K))
    y_c = jax.nn.sigmoid(g_c) * (cb * conv)

    return (y_m.astype(h.dtype) + y_c) @ w_out


def hier_moe(h, w_group, b_group, w_router, b_router, w_gate, w_up, w_down):
    Bsz, L, D = h.shape
    f32 = jnp.float32
    t = h.reshape(-1, D)
    gl = (t @ w_group).astype(f32) + b_group.astype(f32)
    gp = jax.nn.softmax(gl, axis=-1)
    g_val, g_idx = lax.top_k(gp, 1)
    el = ((t @ w_router).astype(f32) + b_router.astype(f32)).reshape(-1, N_GROUPS, EXP_PER_GROUP)
    el_sel = jnp.take_along_axis(el, g_idx[:, :, None], axis=1)[:, 0]
    top_v, top_i = lax.top_k(el_sel, TOP_K)
    w = jax.nn.softmax(top_v, axis=-1) * g_val
    eid = g_idx * EXP_PER_GROUP + top_i
    gate = jnp.sum(jax.nn.one_hot(eid, N_EXPERTS, dtype=f32) * w[..., None], axis=1)
    hid = jax.nn.silu(jnp.einsum('nd,edf->nef', t, w_gate)) * jnp.einsum('nd,edf->nef', t, w_up)
    hid = hid * gate.astype(hid.dtype)[:, :, None]
    y = jnp.einsum('nef,efd->nd', hid, w_down)
    return y.reshape(Bsz, L, D)


def setup_inputs(seed: int = 0) -> dict:
    key = jax.random.key(seed)
    ks = jax.random.split(key, 18)

    def nrm(k, shape, scale):
        return jax.random.normal(k, shape, jnp.float32) * scale

    b_if = jnp.concatenate([nrm(ks[4], (DEPTH, M_HEADS), 0.1),
                            F_BIAS + nrm(ks[5], (DEPTH, M_HEADS), 0.5)], axis=-1)
    return {
        'x': nrm(ks[0], (BATCH, SEQ, D_MODEL), 1.0),
        'meta_tokens': nrm(ks[1], (N_META, D_MODEL), 1.0),
        'norm_mix': 1.0 + nrm(ks[2], (DEPTH, D_MODEL), 0.05),
        'w_in': nrm(ks[3], (DEPTH, D_MODEL, IN_TOTAL), D_MODEL ** -0.5),
        'b_if': b_if,
        'conv_w': nrm(ks[6], (DEPTH, CONV_K, CONV_WIDTH), CONV_K ** -0.5),
        'mh_gain': 1.0 + nrm(ks[7], (DEPTH, V_W), 0.05),
        'w_out': nrm(ks[8], (DEPTH, D_MODEL, D_MODEL), D_MODEL ** -0.5),
        'norm_ffn': 1.0 + nrm(ks[9], (DEPTH, D_MODEL), 0.05),
        'w_group': nrm(ks[10], (DEPTH, D_MODEL, N_GROUPS), D_MODEL ** -0.5),
        'b_group': nrm(ks[11], (DEPTH, N_GROUPS), 0.01),
        'w_router': nrm(ks[12], (DEPTH, D_MODEL, N_EXPERTS), D_MODEL ** -0.5),
        'b_router': nrm(ks[13], (DEPTH, N_EXPERTS), 0.01),
        'w_gate': nrm(ks[14], (DEPTH, N_EXPERTS, D_MODEL, D_EXPERT), D_MODEL ** -0.5),
        'w_up': nrm(ks[15], (DEPTH, N_EXPERTS, D_MODEL, D_EXPERT), D_MODEL ** -0.5),
        'w_down': nrm(ks[16], (DEPTH, N_EXPERTS, D_EXPERT, D_MODEL), D_EXPERT ** -0.5),
        'norm_final': 1.0 + nrm(ks[17], (D_MODEL,), 0.05),
    }


def reference(x, meta_tokens, norm_mix, w_in, b_if, conv_w, mh_gain, w_out, norm_ffn,
              w_group, b_group, w_router, b_router, w_gate, w_up, w_down, norm_final):
    Bsz = x.shape[0]
    meta = jnp.broadcast_to(meta_tokens[None], (Bsz, N_META, D_MODEL)).astype(x.dtype)
    h = jnp.concatenate([meta, x], axis=1)
    for l in range(DEPTH):
        h = h + hybrid_mixer(rmsnorm(h, norm_mix[l]), w_in[l], b_if[l], conv_w[l], mh_gain[l], w_out[l])
        h = h + hier_moe(rmsnorm(h, norm_ffn[l]), w_group[l], b_group[l], w_router[l], b_router[l],
                         w_gate[l], w_up[l], w_down[l])
    return rmsnorm(h, norm_final)[:, N_META:]
```

```python
import functools

import jax
import jax.numpy as jnp
import numpy as np
from jax import lax
from jax.experimental import pallas as pl
from jax.experimental.pallas import tpu as pltpu

D_MODEL = 1024
N_META = 16
M_HEADS = 4
M_DQK = 128
M_DV = 256
REF_CHUNK = 64
N_GROUPS = 4
EXP_PER_GROUP = 4
N_EXPERTS = N_GROUPS * EXP_PER_GROUP
N_PAIRS = 6
N_BUCKETS = N_GROUPS * N_PAIRS
D_EXPERT = 256
EPS = 1e-6
QK_W = M_HEADS * M_DQK
V_W = M_HEADS * M_DV

LANES = 128
SUBLANES = 8
ROW_W = D_MODEL + LANES
VMEM_LIMIT = 56 * 1024 * 1024

_OFF_Q = 0
_OFF_K = _OFF_Q + QK_W
_OFF_V = _OFF_K + QK_W
_OFF_GM = _OFF_V + V_W
_OFF_CX = _OFF_GM + V_W
_OFF_CB = _OFF_CX + D_MODEL
_OFF_CC = _OFF_CB + D_MODEL
_OFF_GC = _OFF_CC + D_MODEL
MAIN_W = _OFF_GC + D_MODEL

MIX_CHUNK = 64
MIX_ROWS = 256
EXPERT_ROWS = 256

_F32 = jnp.float32
_BF16 = jnp.bfloat16


def _rms(x, g):
    return x * lax.rsqrt(jnp.mean(x * x, axis=-1, keepdims=True) + EPS) * g


def _dot(a, b):
    return jnp.dot(a, b, preferred_element_type=_F32)


def _route(lg):
    g = [lg[:, i:i + 1] for i in range(N_GROUPS)]
    gmax = jnp.maximum(jnp.maximum(g[0], g[1]), jnp.maximum(g[2], g[3]))
    gsum = sum(jnp.exp(gi - gmax) for gi in g)
    g_val = 1.0 / gsum
    g_idx = jnp.where(g[0] == gmax, 0, jnp.where(g[1] == gmax, 1, jnp.where(g[2] == gmax, 2, 3)))
    e = []
    for k in range(EXP_PER_GROUP):
        col = lambda i, k=k: lg[:, N_GROUPS + EXP_PER_GROUP * i + k:N_GROUPS + EXP_PER_GROUP * i + k + 1]
        e.append(jnp.where(g_idx == 0, col(0), jnp.where(g_idx == 1, col(1), jnp.where(g_idx == 2, col(2), col(3)))))
    v1 = jnp.maximum(jnp.maximum(e[0], e[1]), jnp.maximum(e[2], e[3]))
    i1 = jnp.where(e[0] == v1, 0, jnp.where(e[1] == v1, 1, jnp.where(e[2] == v1, 2, 3)))
    neg = jnp.float32(-jnp.inf)
    r = [jnp.where(i1 == k, neg, e[k]) for k in range(EXP_PER_GROUP)]
    v2 = jnp.maximum(jnp.maximum(r[0], r[1]), jnp.maximum(r[2], r[3]))
    i2 = jnp.where((r[0] == v2) & (i1 != 0), 0,
                   jnp.where((r[1] == v2) & (i1 != 1), 1, jnp.where((r[2] == v2) & (i1 != 2), 2, 3)))
    ex = jnp.exp(v2 - v1)
    w1 = g_val / (1.0 + ex)
    w2 = g_val * ex / (1.0 + ex)
    lo = jnp.minimum(i1, i2)
    hi = jnp.maximum(i1, i2)
    w_lo = jnp.where(i1 < i2, w1, w2)
    w_hi = jnp.where(i1 < i2, w2, w1)
    pair = jnp.where(lo == 0, hi - 1, jnp.where(lo == 1, hi + 1, 5))
    bucket = (g_idx * N_PAIRS + pair).astype(_F32)
    lane = lax.broadcasted_iota(jnp.int32, lg.shape, 1)
    return jnp.where(lane == 0, bucket, jnp.where(lane == 1, w_lo, jnp.where(lane == 2, w_hi, 0.0)))


def _mixer_kernel(h_ref, nmix_ref, wmain_ref, wif_ref, bif_ref, convw_ref, gain_ref, wout_ref,
                  nffn_ref, wr_ref, br_ref, c0_ref, n0_ref, m0_ref, u0_ref,
                  out_ref, cN_ref, nN_ref, mN_ref, uN_ref,
                  c_scr, n_scr, m_scr, ubuf, ybuf, *, rows, chunk):
    t = pl.program_id(1)

    @pl.when(t == 0)
    def _():
        c_scr[...] = c0_ref[...]
        n_scr[...] = n0_ref[...]
        m_scr[...] = m0_ref[...]
        ubuf[0:SUBLANES, :] = u0_ref[...]

    h = h_ref[...]
    xn = _rms(h, nmix_ref[...]).astype(_BF16)

    def proj(off, width):
        return _dot(xn, wmain_ref[:, off:off + width])

    u = proj(_OFF_CC, D_MODEL) * proj(_OFF_CX, D_MODEL)
    ubuf[SUBLANES:SUBLANES + rows, :] = u
    conv = (convw_ref[0:1, :] * ubuf[SUBLANES - 2:SUBLANES - 2 + rows, :]
            + convw_ref[1:2, :] * ubuf[SUBLANES - 1:SUBLANES - 1 + rows, :]
            + convw_ref[2:3, :] * u)
    ubuf[0:SUBLANES, :] = ubuf[rows:rows + SUBLANES, :]
    ybuf[...] = jax.nn.sigmoid(proj(_OFF_GC, D_MODEL)) * (proj(_OFF_CB, D_MODEL) * conv)

    gates = _dot(xn, wif_ref[...]) + bif_ref[...]
    lane = lax.broadcasted_iota(jnp.int32, gates.shape, 1)
    is_f = (lane >= M_HEADS) & (lane < 2 * M_HEADS)
    logf = jnp.where(is_f, jax.nn.log_sigmoid(gates), 0.0)
    ri = lax.broadcasted_iota(jnp.int32, (rows, rows), 0)
    ci = lax.broadcasted_iota(jnp.int32, (rows, rows), 1)
    shift = chunk.bit_length() - 1
    same_chunk_le = ((ri >> shift) == (ci >> shift)) & (ci <= ri)
    bcum = jnp.dot(same_chunk_le.astype(_F32), logf, preferred_element_type=_F32,
                   precision=lax.Precision.HIGHEST)
    gates_t = gates.T
    bcum_t = bcum.T

    q_all = (proj(_OFF_Q, QK_W) * (M_DQK ** -0.5)).astype(_BF16)
    k_all = proj(_OFF_K, QK_W)
    v_all = proj(_OFF_V, V_W).astype(_BF16)
    gm_all = proj(_OFF_GM, V_W)

    ti = lax.broadcasted_iota(jnp.int32, (chunk, chunk), 0)
    si = lax.broadcasted_iota(jnp.int32, (chunk, chunk), 1)
    causal = si <= ti

    for c in range(rows // chunk):
        r0 = c * chunk
        for hd in range(M_HEADS):
            q = q_all[r0:r0 + chunk, hd * M_DQK:(hd + 1) * M_DQK]
            kf = k_all[r0:r0 + chunk, hd * M_DQK:(hd + 1) * M_DQK]
            v = v_all[r0:r0 + chunk, hd * M_DV:(hd + 1) * M_DV]
            b_c = bcum[r0:r0 + chunk, M_HEADS + hd:M_HEADS + hd + 1]
            i_c = gates[r0:r0 + chunk, hd:hd + 1]
            b_r = bcum_t[M_HEADS + hd:M_HEADS + hd + 1, r0:r0 + chunk]
            i_r = gates_t[hd:hd + 1, r0:r0 + chunk]
            m_prev = m_scr[hd:hd + 1, 0:1]
            c_prev = c_scr[hd]
            n_prev = n_scr[hd:hd + 1, :]

            log_d = jnp.where(causal, b_c - b_r + i_r, -jnp.inf)
            inter = b_c + m_prev
            m_t = jnp.maximum(inter, jnp.max(log_d, axis=-1, keepdims=True))
            dmat = jnp.exp(log_d - m_t)
            s = lax.dot_general(q, kf.astype(_BF16), (((1,), (1,)), ((), ())),
                                preferred_element_type=_F32) * dmat
            e_inter = jnp.exp(inter - m_t)
            num = _dot(s.astype(_BF16), v) + e_inter * _dot(q, c_prev.astype(_BF16))
            den = (jnp.sum(s, axis=-1, keepdims=True)
                   + e_inter * jnp.sum(q.astype(_F32) * n_prev, axis=-1, keepdims=True))
            hc = num / jnp.maximum(jnp.abs(den), jnp.exp(-m_t))

            b_last = b_c[chunk - 1:chunk, :]
            log_w = b_last - b_c + i_c
            m_new = jnp.maximum(b_last + m_prev, jnp.max(log_w, axis=0, keepdims=True))
            kw = kf * jnp.exp(log_w - m_new)
            decay = jnp.exp(b_last + m_prev - m_new)
            c_scr[hd] = decay * c_prev + lax.dot_general(
                kw.astype(_BF16), v, (((0,), (0,)), ((), ())), preferred_element_type=_F32)
            n_scr[hd:hd + 1, :] = decay * n_prev + jnp.sum(kw, axis=0, keepdims=True)
            m_scr[hd:hd + 1, :] = jnp.broadcast_to(m_new, (1, LANES))

            hn = hc * lax.rsqrt(jnp.mean(hc * hc, axis=-1, keepdims=True) + EPS)
            cols = slice(hd * M_DV, (hd + 1) * M_DV)
            ybuf[r0:r0 + chunk, cols] += (jax.nn.sigmoid(gm_all[r0:r0 + chunk, cols])
                                          * (hn * gain_ref[:, cols]))

    h1 = h + _dot(ybuf[...].astype(_BF16), wout_ref[...])

    lg = jnp.dot(_rms(h1, nffn_ref[...]), wr_ref[...], preferred_element_type=_F32,
                 precision=lax.Precision.HIGHEST) + br_ref[...]
    out_ref[:, 0:D_MODEL] = h1
    out_ref[:, D_MODEL:ROW_W] = _route(lg)

    @pl.when(t == pl.num_programs(1) - 1)
    def _():
        cN_ref[...] = c_scr[...]
        nN_ref[...] = n_scr[...]
        mN_ref[...] = m_scr[...]
        uN_ref[...] = ubuf[0:SUBLANES, :]


def _mixer(h, seqs, seq_len, lw, state):
    rows = min(MIX_ROWS, seq_len)
    chunk = min(MIX_CHUNK, rows)
    steps = seq_len // rows
    const = lambda shape: pl.BlockSpec(shape, lambda b, t: (0,) * len(shape), pipeline_mode=pl.Buffered(1))
    state_shapes = [(M_HEADS, M_DQK, M_DV), (SUBLANES, M_DQK), (SUBLANES, LANES), (SUBLANES, D_MODEL)]
    in_specs = [
        pl.BlockSpec((rows, D_MODEL), lambda b, t: (b * steps + t, 0)),
        const((1, D_MODEL)), const((D_MODEL, MAIN_W)), const((D_MODEL, LANES)), const((1, LANES)),
        const((3, D_MODEL)), const((1, V_W)), const((D_MODEL, D_MODEL)), const((1, D_MODEL)),
        const((D_MODEL, LANES)), const((1, LANES)),
    ] + [const(s) for s in state_shapes]
    out_specs = [pl.BlockSpec((rows, ROW_W), lambda b, t: (b * steps + t, 0))] + [
        pl.BlockSpec(s, lambda b, t, n=len(s): (0,) * n) for s in state_shapes]
    out_shape = [jax.ShapeDtypeStruct((seqs * seq_len, ROW_W), _F32)] + [
        jax.ShapeDtypeStruct(s, _F32) for s in state_shapes]
    outs = pl.pallas_call(
        functools.partial(_mixer_kernel, rows=rows, chunk=chunk),
        grid=(seqs, steps),
        in_specs=in_specs,
        out_specs=out_specs,
        out_shape=out_shape,
        scratch_shapes=[
            pltpu.VMEM((M_HEADS, M_DQK, M_DV), _F32),
            pltpu.VMEM((SUBLANES, M_DQK), _F32),
            pltpu.VMEM((SUBLANES, LANES), _F32),
            pltpu.VMEM((rows + SUBLANES, D_MODEL), _F32),
            pltpu.VMEM((rows, D_MODEL), _F32),
        ],
        compiler_params=pltpu.CompilerParams(
            dimension_semantics=("arbitrary", "arbitrary"), vmem_limit_bytes=VMEM_LIMIT),
        name="mixer",
    )(h, lw["norm_mix"], lw["w_main"], lw["w_if"], lw["b_if"], lw["conv_w"], lw["mh_gain"], lw["w_out"],
      lw["norm_ffn"], lw["w_route"], lw["b_route"], *state)
    return outs[0], tuple(outs[1:])


def _expert_kernel(elo_ref, ehi_ref, nv_ref, inv_hbm, hx_hbm, nffn_ref, nfin_ref,
                   wg_lo, wg_hi, wu_lo, wu_hi, wd_lo, wd_hi, out_hbm,
                   idx_smem, xbuf, obuf, sem_i, sem_g, sem_s, *, tile, final_norm):
    j = pl.program_id(0)
    nv = nv_ref[j]

    def row_in(i):
        return pltpu.make_async_copy(hx_hbm.at[pl.ds(idx_smem[i], 1)], xbuf.at[pl.ds(i, 1)], sem_g)

    def row_out(i):
        return pltpu.make_async_copy(obuf.at[pl.ds(i, 1)], out_hbm.at[pl.ds(idx_smem[i], 1)], sem_s)

    @pl.when(nv > 0)
    def _():
        idx_copy = pltpu.make_async_copy(inv_hbm.at[pl.ds(j * tile, tile)], idx_smem, sem_i)
        idx_copy.start()
        idx_copy.wait()

        def issue_in(i, carry):
            row_in(i).start()
            return carry
        lax.fori_loop(0, tile, issue_in, 0)

        def wait_in(i, carry):
            row_in(i).wait()
            return carry
        lax.fori_loop(0, tile, wait_in, 0)

        hrow = xbuf[:, 0:D_MODEL]
        info = xbuf[:, D_MODEL:ROW_W]
        w_lo = info[:, 1:2]
        w_hi = info[:, 2:3]
        xn = _rms(hrow, nffn_ref[...]).astype(_BF16)
        hid_lo = jax.nn.silu(_dot(xn, wg_lo[0])) * _dot(xn, wu_lo[0]) * w_lo
        hid_hi = jax.nn.silu(_dot(xn, wg_hi[0])) * _dot(xn, wu_hi[0]) * w_hi
        y = hrow + _dot(hid_lo.astype(_BF16), wd_lo[0]) + _dot(hid_hi.astype(_BF16), wd_hi[0])
        if final_norm:
            y = _rms(y, nfin_ref[...])
        obuf[...] = y

        def issue_out(i, carry):
            row_out(i).start()
            return carry
        lax.fori_loop(0, nv, issue_out, 0)

        def wait_out(i, carry):
            row_out(i).wait()
            return carry
        lax.fori_loop(0, nv, wait_out, 0)


_PAIR_LO = np.array([0, 0, 0, 1, 1, 2], np.int32)
_PAIR_HI = np.array([1, 2, 3, 2, 3, 3], np.int32)


def _sort_plan(bucket, n_tokens, tile):
    n_tiles = -(-n_tokens // tile) + N_BUCKETS
    order = jnp.argsort(bucket, stable=True).astype(jnp.int32)
    counts = jnp.sum((bucket[:, None] == jnp.arange(N_BUCKETS, dtype=jnp.int32)[None, :]).astype(jnp.int32), axis=0)
    tiles = (counts + tile - 1) // tile
    tile_end = jnp.cumsum(tiles)
    tile_start = tile_end - tiles
    cnt_start = jnp.cumsum(counts) - counts
    total = tile_end[-1]
    j = jnp.arange(n_tiles, dtype=jnp.int32)
    tb = jnp.minimum(jnp.searchsorted(tile_end, j, side="right").astype(jnp.int32), N_BUCKETS - 1)
    used = j < total
    tb = jnp.where(used, tb, tb[jnp.maximum(total - 1, 0)])
    row0 = (j - tile_start[tb]) * tile
    nv = jnp.where(used, jnp.clip(counts[tb] - row0, 0, tile), 0).astype(jnp.int32)
    src = cnt_start[tb] + row0
    k = jnp.arange(tile, dtype=jnp.int32)
    q = jnp.clip(src[:, None] + k[None, :], 0, n_tokens - 1)
    inv = jnp.where(k[None, :] < nv[:, None], order[q], 0).reshape(-1).astype(jnp.int32)
    grp = tb // N_PAIRS
    pr = tb % N_PAIRS
    e_lo = (grp * EXP_PER_GROUP + jnp.asarray(_PAIR_LO)[pr]).astype(jnp.int32)
    e_hi = (grp * EXP_PER_GROUP + jnp.asarray(_PAIR_HI)[pr]).astype(jnp.int32)
    return e_lo, e_hi, nv, inv, n_tiles


def _experts(hx, lw, norm_final, final_norm):
    n_tokens = hx.shape[0]
    tile = EXPERT_ROWS
    bucket = hx[:, D_MODEL].astype(jnp.int32)
    e_lo, e_hi, nv, inv, n_tiles = _sort_plan(bucket, n_tokens, tile)
    any_spec = pl.BlockSpec(memory_space=pl.ANY)
    vec = pl.BlockSpec((1, D_MODEL), lambda j, lo, hi, nv: (0, 0))
    w_in_lo = pl.BlockSpec((1, D_MODEL, D_EXPERT), lambda j, lo, hi, nv: (lo[j], 0, 0))
    w_in_hi = pl.BlockSpec((1, D_MODEL, D_EXPERT), lambda j, lo, hi, nv: (hi[j], 0, 0))
    w_dn_lo = pl.BlockSpec((1, D_EXPERT, D_MODEL), lambda j, lo, hi, nv: (lo[j], 0, 0))
    w_dn_hi = pl.BlockSpec((1, D_EXPERT, D_MODEL), lambda j, lo, hi, nv: (hi[j], 0, 0))
    return pl.pallas_call(
        functools.partial(_expert_kernel, tile=tile, final_norm=final_norm),
        grid_spec=pltpu.PrefetchScalarGridSpec(
            num_scalar_prefetch=3,
            grid=(n_tiles,),
            in_specs=[any_spec, any_spec, vec, vec, w_in_lo, w_in_hi, w_in_lo, w_in_hi, w_dn_lo, w_dn_hi],
            out_specs=any_spec,
            scratch_shapes=[
                pltpu.SMEM((tile,), jnp.int32),
                pltpu.VMEM((tile, ROW_W), _F32),
                pltpu.VMEM((tile, D_MODEL), _F32),
                pltpu.SemaphoreType.DMA,
                pltpu.SemaphoreType.DMA,
                pltpu.SemaphoreType.DMA,
            ],
        ),
        out_shape=jax.ShapeDtypeStruct((n_tokens, D_MODEL), _F32),
        compiler_params=pltpu.CompilerParams(
            dimension_semantics=("arbitrary",), vmem_limit_bytes=VMEM_LIMIT),
        name="experts",
    )(e_lo, e_hi, nv, inv, hx, lw["norm_ffn"], norm_final,
      lw["w_gate"], lw["w_gate"], lw["w_up"], lw["w_up"], lw["w_down"], lw["w_down"])


def _layer_weights(l, norm_mix, w_in, b_if, conv_w, mh_gain, w_out, norm_ffn,
                   w_group, b_group, w_router, b_router, w_gate, w_up, w_down):
    w = w_in[l]
    o_if = 2 * QK_W + V_W
    o_rest = o_if + 2 * M_HEADS
    w_main = jnp.concatenate([w[:, :o_if], w[:, o_rest:]], axis=1).astype(_BF16)
    w_if = jnp.pad(w[:, o_if:o_rest], ((0, 0), (0, LANES - 2 * M_HEADS))).astype(_BF16)
    pad_lanes = lambda a: jnp.pad(a, ((0, 0), (0, LANES - a.shape[1])))
    return {
        "norm_mix": norm_mix[l][None], "w_main": w_main, "w_if": w_if,
        "b_if": pad_lanes(b_if[l][None]), "conv_w": conv_w[l], "mh_gain": mh_gain[l][None],
        "w_out": w_out[l].astype(_BF16), "norm_ffn": norm_ffn[l][None],
        "w_route": pad_lanes(jnp.concatenate([w_group[l], w_router[l]], axis=1)),
        "b_route": pad_lanes(jnp.concatenate([b_group[l], b_router[l]])[None]),
        "w_gate": w_gate[l].astype(_BF16), "w_up": w_up[l].astype(_BF16), "w_down": w_down[l].astype(_BF16),
    }


def kernel(x, meta_tokens, norm_mix, w_in, b_if, conv_w, mh_gain, w_out, norm_ffn, w_group, b_group,
           w_router, b_router, w_gate, w_up, w_down, norm_final):
    batch, seq, d = x.shape
    depth = w_in.shape[0]
    assert d == D_MODEL and seq % MIX_ROWS == 0 and meta_tokens.shape == (N_META, D_MODEL)
    layers = [_layer_weights(l, norm_mix, w_in, b_if, conv_w, mh_gain, w_out, norm_ffn,
                             w_group, b_group, w_router, b_router, w_gate, w_up, w_down)
              for l in range(depth)]
    nfin = norm_final[None]
    zero_state = (jnp.zeros((M_HEADS, M_DQK, M_DV), _F32), jnp.zeros((SUBLANES, M_DQK), _F32),
                  jnp.zeros((SUBLANES, LANES), _F32), jnp.zeros((SUBLANES, D_MODEL), _F32))

    hm = jnp.concatenate([jnp.zeros((REF_CHUNK - N_META, D_MODEL), _F32), meta_tokens.astype(_F32)], axis=0)
    states = []
    for l in range(depth):
        hmx, st = _mixer(hm, 1, REF_CHUNK, layers[l], zero_state)
        states.append(st)
        if l + 1 < depth:
            hm = _experts(hmx, layers[l], nfin, False)

    h = x.reshape(batch * seq, D_MODEL)
    for l in range(depth):
        hx, _ = _mixer(h, batch, seq, layers[l], states[l])
        h = _experts(hx, layers[l], nfin, l + 1 == depth)
    return h.reshape(batch, seq, D_MODEL)
```

```python
import functools

import jax
import jax.numpy as jnp
import numpy as np
from jax import lax
from jax.experimental import pallas as pl
from jax.experimental.pallas import tpu as pltpu

D_MODEL = 1024
N_META = 16
M_HEADS = 4
M_DQK = 128
M_DV = 256
REF_CHUNK = 64
N_GROUPS = 4
EXP_PER_GROUP = 4
N_PAIRS = 6
N_BUCKETS = N_GROUPS * N_PAIRS
D_EXPERT = 256
EPS = 1e-6
QK_W = M_HEADS * M_DQK
V_W = M_HEADS * M_DV

LANES = 128
SUBLANES = 8
ROW_W = D_MODEL + LANES
VMEM_LIMIT = 56 * 1024 * 1024

_OFF_Q = 0
_OFF_K = _OFF_Q + QK_W
_OFF_V = _OFF_K + QK_W
_OFF_GM = _OFF_V + V_W
_OFF_CX = _OFF_GM + V_W
_OFF_CB = _OFF_CX + D_MODEL
_OFF_CC = _OFF_CB + D_MODEL
_OFF_GC = _OFF_CC + D_MODEL
MAIN_W = _OFF_GC + D_MODEL

MIX_CHUNK = 256
MIX_ROWS = 256
EXPERT_ROWS = 256
NORM_ROWS = 256
ISSUE_UNROLL = 8

_F32 = jnp.float32
_BF16 = jnp.bfloat16


def _rms(x, g):
    return x * lax.rsqrt(jnp.mean(x * x, axis=-1, keepdims=True) + EPS) * g


def _dot(a, b):
    return jnp.dot(a, b, preferred_element_type=_F32)


def _split2(x):
    hi = x.astype(_BF16)
    return hi, (x - hi.astype(_F32)).astype(_BF16)


MIN_IDX_TILE = 128


def _idx_tiles(idx, rows):
    idx_rows = max(rows, MIN_IDX_TILE)
    if idx_rows == rows:
        return idx, idx_rows
    return jnp.pad(idx.reshape(-1, rows), ((0, 0), (0, idx_rows - rows))).reshape(-1), idx_rows


def _gather_step(step, n_steps, idx_hbm, src_hbm, idx_smem, buf, sem_i, sem_g, rows):
    slot = step % 2
    n_steps = jnp.asarray(n_steps, jnp.int32)
    idx_rows = idx_smem.shape[1]

    def idx_copy(s):
        return pltpu.make_async_copy(idx_hbm.at[pl.ds(s * idx_rows, idx_rows)], idx_smem.at[s % 2],
                                     sem_i.at[s % 2])

    def issue(s):
        def body(i, carry):
            pltpu.make_async_copy(src_hbm.at[pl.ds(idx_smem[s % 2, i], 1)],
                                  buf.at[s % 2, pl.ds(i, 1)], sem_g.at[s % 2]).start()
            return carry
        lax.fori_loop(0, rows, body, 0, unroll=ISSUE_UNROLL)

    @pl.when(step == 0)
    def _():
        idx_copy(0).start()
        idx_copy(0).wait()
        issue(0)

        @pl.when(n_steps > 1)
        def _():
            idx_copy(step + 1).start()

    @pl.when(step + 1 < n_steps)
    def _():
        idx_copy(step + 1).wait()
        issue(step + 1)

        @pl.when(step + 2 < n_steps)
        def _():
            idx_copy(step + 2).start()

    pltpu.make_async_copy(src_hbm.at[pl.ds(0, rows)], buf.at[slot], sem_g.at[slot]).wait()
    return slot


def _route(lg):
    g = [lg[:, i:i + 1] for i in range(N_GROUPS)]
    gmax = jnp.maximum(jnp.maximum(g[0], g[1]), jnp.maximum(g[2], g[3]))
    gsum = sum(jnp.exp(gi - gmax) for gi in g)
    g_val = 1.0 / gsum
    g_idx = jnp.where(g[0] == gmax, 0, jnp.where(g[1] == gmax, 1, jnp.where(g[2] == gmax, 2, 3)))
    e = []
    for k in range(EXP_PER_GROUP):
        col = lambda i, k=k: lg[:, N_GROUPS + EXP_PER_GROUP * i + k:N_GROUPS + EXP_PER_GROUP * i + k + 1]
        e.append(jnp.where(g_idx == 0, col(0), jnp.where(g_idx == 1, col(1), jnp.where(g_idx == 2, col(2), col(3)))))
    v1 = jnp.maximum(jnp.maximum(e[0], e[1]), jnp.maximum(e[2], e[3]))
    i1 = jnp.where(e[0] == v1, 0, jnp.where(e[1] == v1, 1, jnp.where(e[2] == v1, 2, 3)))
    neg = jnp.float32(-jnp.inf)
    r = [jnp.where(i1 == k, neg, e[k]) for k in range(EXP_PER_GROUP)]
    v2 = jnp.maximum(jnp.maximum(r[0], r[1]), jnp.maximum(r[2], r[3]))
    i2 = jnp.where((r[0] == v2) & (i1 != 0), 0,
                   jnp.where((r[1] == v2) & (i1 != 1), 1, jnp.where((r[2] == v2) & (i1 != 2), 2, 3)))
    ex = jnp.exp(v2 - v1)
    w1 = g_val / (1.0 + ex)
    w2 = g_val * ex / (1.0 + ex)
    lo = jnp.minimum(i1, i2)
    hi = jnp.maximum(i1, i2)
    w_lo = jnp.where(i1 < i2, w1, w2)
    w_hi = jnp.where(i1 < i2, w2, w1)
    pair = jnp.where(lo == 0, hi - 1, jnp.where(lo == 1, hi + 1, 5))
    bucket = (g_idx * N_PAIRS + pair).astype(_F32)
    lane = lax.broadcasted_iota(jnp.int32, lg.shape, 1)
    return jnp.where(lane == 0, bucket, jnp.where(lane == 1, w_lo, jnp.where(lane == 2, w_hi, 0.0)))


def _mixer_kernel(*refs, rows, chunk, gathered):
    if gathered:
        idx_hbm, src_hbm = refs[0], refs[1]
    else:
        h_ref = refs[0]
    n_in = 2 if gathered else 1
    (nmix_ref, wmain_ref, wif_ref, bif_ref, convw_ref, gain_ref, wout_ref, nffn_ref, wrhi_ref, wrlo_ref,
     br_ref, c0_ref, n0_ref, m0_ref, u0_ref) = refs[n_in:n_in + 15]
    out_ref, info_ref, cN_ref, nN_ref, mN_ref, uN_ref = refs[n_in + 15:n_in + 21]
    c_scr, n_scr, m_scr, ubuf, ybuf = refs[n_in + 21:n_in + 26]
    t = pl.program_id(1)

    @pl.when(t == 0)
    def _():
        c_scr[...] = c0_ref[...]
        n_scr[...] = n0_ref[...]
        m_scr[...] = m0_ref[...]
        ubuf[0:SUBLANES, :] = u0_ref[...]

    if gathered:
        idx_smem, hbuf, sem_i, sem_g = refs[n_in + 26:n_in + 30]
        n_steps = pl.num_programs(0) * pl.num_programs(1)
        slot = _gather_step(pl.program_id(0) * pl.num_programs(1) + t, n_steps,
                            idx_hbm, src_hbm, idx_smem, hbuf, sem_i, sem_g, rows)
        h = hbuf[slot]
    else:
        h = h_ref[...]
    xn = _rms(h, nmix_ref[...]).astype(_BF16)

    def proj(off, width):
        return _dot(xn, wmain_ref[:, off:off + width])

    u = proj(_OFF_CC, D_MODEL) * proj(_OFF_CX, D_MODEL)
    ubuf[SUBLANES:SUBLANES + rows, :] = u
    conv = (convw_ref[0:1, :] * ubuf[SUBLANES - 2:SUBLANES - 2 + rows, :]
            + convw_ref[1:2, :] * ubuf[SUBLANES - 1:SUBLANES - 1 + rows, :]
            + convw_ref[2:3, :] * u)
    ubuf[0:SUBLANES, :] = ubuf[rows:rows + SUBLANES, :]
    ybuf[...] = jax.nn.sigmoid(proj(_OFF_GC, D_MODEL)) * (proj(_OFF_CB, D_MODEL) * conv)

    gates = _dot(xn, wif_ref[...]) + bif_ref[...]
    lane = lax.broadcasted_iota(jnp.int32, gates.shape, 1)
    is_f = (lane >= M_HEADS) & (lane < 2 * M_HEADS)
    logf = jnp.where(is_f, jax.nn.log_sigmoid(gates), 0.0)
    ri = lax.broadcasted_iota(jnp.int32, (rows, rows), 0)
    ci = lax.broadcasted_iota(jnp.int32, (rows, rows), 1)
    shift = chunk.bit_length() - 1
    tri = (((ri >> shift) == (ci >> shift)) & (ci <= ri)).astype(_BF16)
    f_hi = logf.astype(_BF16)
    f_r = logf - f_hi.astype(_F32)
    f_mid = f_r.astype(_BF16)
    f_lo = (f_r - f_mid.astype(_F32)).astype(_BF16)
    bcum = _dot(tri, f_hi) + _dot(tri, f_mid) + _dot(tri, f_lo)
    gates_t = gates.T
    bcum_t = bcum.T

    q_all = (proj(_OFF_Q, QK_W) * (M_DQK ** -0.5)).astype(_BF16)
    k_all = proj(_OFF_K, QK_W)
    v_all = proj(_OFF_V, V_W).astype(_BF16)
    gm_all = proj(_OFF_GM, V_W)

    ti = lax.broadcasted_iota(jnp.int32, (chunk, chunk), 0)
    si = lax.broadcasted_iota(jnp.int32, (chunk, chunk), 1)
    causal = si <= ti

    for c in range(rows // chunk):
        r0 = c * chunk
        for hd in range(M_HEADS):
            q = q_all[r0:r0 + chunk, hd * M_DQK:(hd + 1) * M_DQK]
            kf = k_all[r0:r0 + chunk, hd * M_DQK:(hd + 1) * M_DQK]
            v = v_all[r0:r0 + chunk, hd * M_DV:(hd + 1) * M_DV]
            b_c = bcum[r0:r0 + chunk, M_HEADS + hd:M_HEADS + hd + 1]
            i_c = gates[r0:r0 + chunk, hd:hd + 1]
            b_r = bcum_t[M_HEADS + hd:M_HEADS + hd + 1, r0:r0 + chunk]
            i_r = gates_t[hd:hd + 1, r0:r0 + chunk]
            m_prev = m_scr[hd:hd + 1, 0:1]
            c_prev = c_scr[hd]
            n_prev = n_scr[hd:hd + 1, :]

            log_d = jnp.where(causal, b_c - b_r + i_r, -jnp.inf)
            inter = b_c + m_prev
            m_t = jnp.maximum(inter, jnp.max(log_d, axis=-1, keepdims=True))
            dmat = jnp.exp(log_d - m_t)
            s = lax.dot_general(q, kf.astype(_BF16), (((1,), (1,)), ((), ())),
                                preferred_element_type=_F32) * dmat
            e_inter = jnp.exp(inter - m_t)
            num = _dot(s.astype(_BF16), v) + e_inter * _dot(q, c_prev.astype(_BF16))
            den = (jnp.sum(s, axis=-1, keepdims=True)
                   + e_inter * jnp.sum(q.astype(_F32) * n_prev, axis=-1, keepdims=True))
            hc = num / jnp.maximum(jnp.abs(den), jnp.exp(-m_t))

            b_last = b_c[chunk - 1:chunk, :]
            log_w = b_last - b_c + i_c
            m_new = jnp.maximum(b_last + m_prev, jnp.max(log_w, axis=0, keepdims=True))
            kw = kf * jnp.exp(log_w - m_new)
            decay = jnp.exp(b_last + m_prev - m_new)
            c_scr[hd] = decay * c_prev + lax.dot_general(
                kw.astype(_BF16), v, (((0,), (0,)), ((), ())), preferred_element_type=_F32)
            n_scr[hd:hd + 1, :] = decay * n_prev + jnp.sum(kw, axis=0, keepdims=True)
            m_scr[hd:hd + 1, :] = jnp.broadcast_to(m_new, (1, LANES))

            hn = hc * lax.rsqrt(jnp.mean(hc * hc, axis=-1, keepdims=True) + EPS)
            cols = slice(hd * M_DV, (hd + 1) * M_DV)
            ybuf[r0:r0 + chunk, cols] += (jax.nn.sigmoid(gm_all[r0:r0 + chunk, cols])
                                          * (hn * gain_ref[:, cols]))

    h1 = h + _dot(ybuf[...].astype(_BF16), wout_ref[...])

    x_hi, x_lo = _split2(_rms(h1, nffn_ref[...]))
    lg = (_dot(x_hi, wrhi_ref[...]) + _dot(x_lo, wrhi_ref[...]) + _dot(x_hi, wrlo_ref[...])) + br_ref[...]
    info = _route(lg)
    out_ref[:, 0:D_MODEL] = h1
    out_ref[:, D_MODEL:ROW_W] = info
    info_ref[0] = info.T[0:SUBLANES, :]

    @pl.when(t == pl.num_programs(1) - 1)
    def _():
        cN_ref[...] = c_scr[...]
        nN_ref[...] = n_scr[...]
        mN_ref[...] = m_scr[...]
        uN_ref[...] = ubuf[0:SUBLANES, :]


def _mixer(h, seqs, seq_len, lw, state, pos=None):
    rows = min(MIX_ROWS, seq_len)
    chunk = min(MIX_CHUNK, rows)
    steps = seq_len // rows
    gathered = pos is not None
    const = lambda shape: pl.BlockSpec(shape, lambda b, t: (0,) * len(shape), pipeline_mode=pl.Buffered(1))
    any_spec = pl.BlockSpec(memory_space=pl.ANY)
    state_shapes = [(M_HEADS, M_DQK, M_DV), (SUBLANES, M_DQK), (SUBLANES, LANES), (SUBLANES, D_MODEL)]
    if gathered:
        idx, idx_rows = _idx_tiles(pos, rows)
        in_specs, args = [any_spec, any_spec], [idx, h]
    else:
        in_specs, args = [pl.BlockSpec((rows, D_MODEL), lambda b, t: (b * steps + t, 0))], [h]
    in_specs += [
        const((1, D_MODEL)), const((D_MODEL, MAIN_W)), const((D_MODEL, LANES)), const((1, LANES)),
        const((3, D_MODEL)), const((1, V_W)), const((D_MODEL, D_MODEL)), const((1, D_MODEL)),
        const((D_MODEL, LANES)), const((D_MODEL, LANES)), const((1, LANES)),
    ] + [const(s) for s in state_shapes]
    out_specs = [pl.BlockSpec((rows, ROW_W), lambda b, t: (b * steps + t, 0)),
                 pl.BlockSpec((1, SUBLANES, rows), lambda b, t: (b * steps + t, 0, 0))] + [
        pl.BlockSpec(s, lambda b, t, n=len(s): (0,) * n) for s in state_shapes]
    out_shape = [jax.ShapeDtypeStruct((seqs * seq_len, ROW_W), _F32),
                 jax.ShapeDtypeStruct((seqs * steps, SUBLANES, rows), _F32)] + [
        jax.ShapeDtypeStruct(s, _F32) for s in state_shapes]
    scratch = [
        pltpu.VMEM((M_HEADS, M_DQK, M_DV), _F32),
        pltpu.VMEM((SUBLANES, M_DQK), _F32),
        pltpu.VMEM((SUBLANES, LANES), _F32),
        pltpu.VMEM((rows + SUBLANES, D_MODEL), _F32),
        pltpu.VMEM((rows, D_MODEL), _F32),
    ]
    if gathered:
        scratch += [pltpu.SMEM((2, idx_rows), jnp.int32), pltpu.VMEM((2, rows, D_MODEL), _F32),
                    pltpu.SemaphoreType.DMA((2,)), pltpu.SemaphoreType.DMA((2,))]
    outs = pl.pallas_call(
        functools.partial(_mixer_kernel, rows=rows, chunk=chunk, gathered=gathered),
        grid=(seqs, steps),
        in_specs=in_specs,
        out_specs=out_specs,
        out_shape=out_shape,
        scratch_shapes=scratch,
        compiler_params=pltpu.CompilerParams(
            dimension_semantics=("arbitrary", "arbitrary"), vmem_limit_bytes=VMEM_LIMIT),
        name="mixer",
    )(*args, lw["norm_mix"], lw["w_main"], lw["w_if"], lw["b_if"], lw["conv_w"], lw["mh_gain"], lw["w_out"],
      lw["norm_ffn"], lw["w_route_hi"], lw["w_route_lo"], lw["b_route"], *state)
    return outs[0], outs[1], tuple(outs[2:])


def _expert_kernel(elo_ref, ehi_ref, nused_ref, inv_hbm, hx_hbm, nffn_ref,
                   wg_lo, wg_hi, wu_lo, wu_hi, wd_lo, wd_hi, out_ref,
                   idx_smem, xbuf, sem_i, sem_g, *, tile):
    j = pl.program_id(0)
    n_used = nused_ref[0]

    @pl.when(j < n_used)
    def _():
        slot = _gather_step(j, n_used, inv_hbm, hx_hbm, idx_smem, xbuf, sem_i, sem_g, tile)
        hrow = xbuf[slot, :, 0:D_MODEL]
        info = xbuf[slot, :, D_MODEL:ROW_W]
        w_lo = info[:, 1:2]
        w_hi = info[:, 2:3]
        xn = _rms(hrow, nffn_ref[...]).astype(_BF16)
        hid_lo = jax.nn.silu(_dot(xn, wg_lo[0])) * _dot(xn, wu_lo[0]) * w_lo
        hid_hi = jax.nn.silu(_dot(xn, wg_hi[0])) * _dot(xn, wu_hi[0]) * w_hi
        out_ref[...] = hrow + _dot(hid_lo.astype(_BF16), wd_lo[0]) + _dot(hid_hi.astype(_BF16), wd_hi[0])

    @pl.when(j >= n_used)
    def _():
        out_ref[...] = jnp.zeros_like(out_ref)


_PAIR_LO = np.array([0, 0, 0, 1, 1, 2], np.int32)
_PAIR_HI = np.array([1, 2, 3, 2, 3, 3], np.int32)


def _sort_plan(bucket, n_tokens, tile):
    n_tiles = -(-n_tokens // tile) + N_BUCKETS
    ids = jnp.arange(N_BUCKETS, dtype=jnp.int32)
    order = jnp.argsort(bucket, stable=True).astype(jnp.int32)
    rank = jnp.argsort(order).astype(jnp.int32)
    counts = jnp.sum((bucket[:, None] == ids[None, :]).astype(jnp.int32), axis=0)
    tiles = (counts + tile - 1) // tile
    tile_end = jnp.cumsum(tiles)
    tile_start = tile_end - tiles
    cnt_start = jnp.cumsum(counts) - counts
    n_used = tile_end[-1]
    shift = tile_start * tile - cnt_start
    pos = rank + jnp.sum(jnp.where(bucket[:, None] == ids[None, :], shift[None, :], 0), axis=1)
    j = jnp.arange(n_tiles, dtype=jnp.int32)
    last = jnp.maximum(n_used - 1, 0)
    jj = jnp.minimum(j, last)
    tb = jnp.sum((jj[:, None] >= tile_end[None, :]).astype(jnp.int32), axis=1)
    r = jnp.arange(n_tiles * tile, dtype=jnp.int32)
    rb = jnp.repeat(tb, tile)
    rsel = rb[:, None] == ids[None, :]
    slot = r - jnp.sum(jnp.where(rsel, shift[None, :], 0), axis=1)
    slot_end = jnp.sum(jnp.where(rsel, (cnt_start + counts)[None, :], 0), axis=1)
    valid = (slot < slot_end) & (jnp.repeat(j, tile) < n_used)
    inv = jnp.where(valid, order[jnp.clip(slot, 0, n_tokens - 1)], r % n_tokens).astype(jnp.int32)
    grp = tb // N_PAIRS
    pr = tb % N_PAIRS
    e_lo = (grp * EXP_PER_GROUP + jnp.asarray(_PAIR_LO)[pr]).astype(jnp.int32)
    e_hi = (grp * EXP_PER_GROUP + jnp.asarray(_PAIR_HI)[pr]).astype(jnp.int32)
    return e_lo, e_hi, n_used.reshape(1).astype(jnp.int32), inv, pos.astype(jnp.int32), n_tiles


def _experts(hx, info, lw):
    n_tokens = hx.shape[0]
    tile = min(EXPERT_ROWS, n_tokens)
    bucket = info[:, 0, :].reshape(n_tokens).astype(jnp.int32)
    e_lo, e_hi, n_used, inv, pos, n_tiles = _sort_plan(bucket, n_tokens, tile)
    inv, idx_rows = _idx_tiles(inv, tile)
    any_spec = pl.BlockSpec(memory_space=pl.ANY)
    vec = pl.BlockSpec((1, D_MODEL), lambda j, lo, hi, nu: (0, 0))
    w_in_lo = pl.BlockSpec((1, D_MODEL, D_EXPERT), lambda j, lo, hi, nu: (lo[j], 0, 0))
    w_in_hi = pl.BlockSpec((1, D_MODEL, D_EXPERT), lambda j, lo, hi, nu: (hi[j], 0, 0))
    w_dn_lo = pl.BlockSpec((1, D_EXPERT, D_MODEL), lambda j, lo, hi, nu: (lo[j], 0, 0))
    w_dn_hi = pl.BlockSpec((1, D_EXPERT, D_MODEL), lambda j, lo, hi, nu: (hi[j], 0, 0))
    ys = pl.pallas_call(
        functools.partial(_expert_kernel, tile=tile),
        grid_spec=pltpu.PrefetchScalarGridSpec(
            num_scalar_prefetch=3,
            grid=(n_tiles,),
            in_specs=[any_spec, any_spec, vec, w_in_lo, w_in_hi, w_in_lo, w_in_hi, w_dn_lo, w_dn_hi],
            out_specs=pl.BlockSpec((tile, D_MODEL), lambda j, lo, hi, nu: (j, 0)),
            scratch_shapes=[
                pltpu.SMEM((2, idx_rows), jnp.int32),
                pltpu.VMEM((2, tile, ROW_W), _F32),
                pltpu.SemaphoreType.DMA((2,)),
                pltpu.SemaphoreType.DMA((2,)),
            ],
        ),
        out_shape=jax.ShapeDtypeStruct((n_tiles * tile, D_MODEL), _F32),
        compiler_params=pltpu.CompilerParams(
            dimension_semantics=("arbitrary",), vmem_limit_bytes=VMEM_LIMIT),
        name="experts",
    )(e_lo, e_hi, n_used, inv, hx, lw["norm_ffn"],
      lw["w_gate"], lw["w_gate"], lw["w_up"], lw["w_up"], lw["w_down"], lw["w_down"])
    return ys, pos


def _final_norm_kernel(pos_hbm, ys_hbm, g_ref, out_ref, idx_smem, buf, sem_i, sem_g, *, rows):
    slot = _gather_step(pl.program_id(0), pl.num_programs(0), pos_hbm, ys_hbm, idx_smem, buf, sem_i, sem_g, rows)
    out_ref[...] = _rms(buf[slot], g_ref[...])


def _final_norm(ys, pos, g):
    n_tokens = pos.shape[0]
    rows = NORM_ROWS
    pos, idx_rows = _idx_tiles(pos, rows)
    any_spec = pl.BlockSpec(memory_space=pl.ANY)
    return pl.pallas_call(
        functools.partial(_final_norm_kernel, rows=rows),
        grid=(n_tokens // rows,),
        in_specs=[any_spec, any_spec, pl.BlockSpec((1, D_MODEL), lambda i: (0, 0))],
        out_specs=pl.BlockSpec((rows, D_MODEL), lambda i: (i, 0)),
        out_shape=jax.ShapeDtypeStruct((n_tokens, D_MODEL), _F32),
        scratch_shapes=[pltpu.SMEM((2, idx_rows), jnp.int32), pltpu.VMEM((2, rows, D_MODEL), _F32),
                        pltpu.SemaphoreType.DMA((2,)), pltpu.SemaphoreType.DMA((2,))],
        compiler_params=pltpu.CompilerParams(dimension_semantics=("arbitrary",)),
        name="final_norm",
    )(pos, ys, g)


def _layer_weights(l, norm_mix, w_in, b_if, conv_w, mh_gain, w_out, norm_ffn,
                   w_group, b_group, w_router, b_router, w_gate, w_up, w_down):
    w = w_in[l]
    o_if = 2 * QK_W + V_W
    o_rest = o_if + 2 * M_HEADS
    w_main = jnp.concatenate([w[:, :o_if], w[:, o_rest:]], axis=1).astype(_BF16)
    w_if = jnp.pad(w[:, o_if:o_rest], ((0, 0), (0, LANES - 2 * M_HEADS))).astype(_BF16)
    pad_lanes = lambda a: jnp.pad(a, ((0, 0), (0, LANES - a.shape[1])))
    w_route = pad_lanes(jnp.concatenate([w_group[l], w_router[l]], axis=1))
    w_route_hi = w_route.astype(_BF16)
    return {
        "norm_mix": norm_mix[l][None], "w_main": w_main, "w_if": w_if,
        "b_if": pad_lanes(b_if[l][None]), "conv_w": conv_w[l], "mh_gain": mh_gain[l][None],
        "w_out": w_out[l].astype(_BF16), "norm_ffn": norm_ffn[l][None],
        "w_route_hi": w_route_hi, "w_route_lo": (w_route - w_route_hi.astype(_F32)).astype(_BF16),
        "b_route": pad_lanes(jnp.concatenate([b_group[l], b_router[l]])[None]),
        "w_gate": w_gate[l].astype(_BF16), "w_up": w_up[l].astype(_BF16), "w_down": w_down[l].astype(_BF16),
    }


def kernel(x, meta_tokens, norm_mix, w_in, b_if, conv_w, mh_gain, w_out, norm_ffn, w_group, b_group,
           w_router, b_router, w_gate, w_up, w_down, norm_final):
    batch, seq, d = x.shape
    depth = w_in.shape[0]
    assert d == D_MODEL and seq % MIX_ROWS == 0 and meta_tokens.shape == (N_META, D_MODEL)
    assert (batch * seq) % NORM_ROWS == 0
    layers = [_layer_weights(l, norm_mix, w_in, b_if, conv_w, mh_gain, w_out, norm_ffn,
                             w_group, b_group, w_router, b_router, w_gate, w_up, w_down)
              for l in range(depth)]
    zero_state = (jnp.zeros((M_HEADS, M_DQK, M_DV), _F32), jnp.zeros((SUBLANES, M_DQK), _F32),
                  jnp.zeros((SUBLANES, LANES), _F32), jnp.zeros((SUBLANES, D_MODEL), _F32))

    hm = jnp.concatenate([jnp.zeros((REF_CHUNK - N_META, D_MODEL), _F32), meta_tokens.astype(_F32)], axis=0)
    pos = None
    states = []
    for l in range(depth):
        hmx, info, st = _mixer(hm, 1, REF_CHUNK, layers[l], zero_state, pos)
        states.append(st)
        if l + 1 < depth:
            hm, pos = _experts(hmx, info, layers[l])

    h = x.reshape(batch * seq, D_MODEL)
    pos = None
    for l in range(depth):
        hx, info, _ = _mixer(h, batch, seq, layers[l], states[l], pos)
        h, pos = _experts(hx, info, layers[l])
    return _final_norm(h, pos, norm_final[None]).reshape(batch, seq, D_MODEL)
```

```python
import functools

import jax
import jax.numpy as jnp
import numpy as np
from jax import lax
from jax.experimental import pallas as pl
from jax.experimental.pallas import tpu as pltpu

D_MODEL = 1024
N_META = 16
M_HEADS = 4
M_DQK = 128
M_DV = 256
REF_CHUNK = 64
N_GROUPS = 4
EXP_PER_GROUP = 4
N_PAIRS = 6
N_BUCKETS = N_GROUPS * N_PAIRS
D_EXPERT = 256
EPS = 1e-6
QK_W = M_HEADS * M_DQK
V_W = M_HEADS * M_DV

LANES = 128
SUBLANES = 8
ROW_W = D_MODEL + LANES
VMEM_LIMIT = 56 * 1024 * 1024

_OFF_Q = 0
_OFF_K = _OFF_Q + QK_W
_OFF_V = _OFF_K + QK_W
_OFF_GM = _OFF_V + V_W
_OFF_CX = _OFF_GM + V_W
_OFF_CB = _OFF_CX + D_MODEL
_OFF_CC = _OFF_CB + D_MODEL
_OFF_GC = _OFF_CC + D_MODEL
MAIN_W = _OFF_GC + D_MODEL

MIX_CHUNK = 256
MIX_ROWS = 256
EXPERT_ROWS = 256
NORM_ROWS = 256

_F32 = jnp.float32
_BF16 = jnp.bfloat16


def _rms(x, g):
    return x * lax.rsqrt(jnp.mean(x * x, axis=-1, keepdims=True) + EPS) * g


def _dot(a, b):
    return jnp.dot(a, b, preferred_element_type=_F32)


def _split2(x):
    hi = x.astype(_BF16)
    return hi, (x - hi.astype(_F32)).astype(_BF16)


MIN_IDX_TILE = 128


def _idx_tiles(idx, rows):
    idx_rows = max(rows, MIN_IDX_TILE)
    if idx_rows == rows:
        return idx, idx_rows
    return jnp.pad(idx.reshape(-1, rows), ((0, 0), (0, idx_rows - rows))).reshape(-1), idx_rows


def _row_buffer(rows, width):
    return pltpu.VMEM((2, rows // SUBLANES, SUBLANES, width), _F32)


def _gather_step(step, n_steps, idx_hbm, src_hbm, idx_smem, buf, sem_i, sem_g, rows):
    slot = step % 2
    n_steps = jnp.asarray(n_steps, jnp.int32)
    idx_rows = idx_smem.shape[1]

    def idx_copy(s):
        return pltpu.make_async_copy(idx_hbm.at[pl.ds(s * idx_rows, idx_rows)], idx_smem.at[s % 2],
                                     sem_i.at[s % 2])

    def issue(s):
        def body(g, carry):
            for k in range(SUBLANES):
                pltpu.make_async_copy(src_hbm.at[idx_smem[s % 2, g * SUBLANES + k]],
                                      buf.at[s % 2, g, pl.ds(k, 1)], sem_g.at[s % 2]).start()
            return carry
        lax.fori_loop(0, rows // SUBLANES, body, 0)

    @pl.when(step == 0)
    def _():
        idx_copy(0).start()
        idx_copy(0).wait()
        issue(0)

        @pl.when(n_steps > 1)
        def _():
            idx_copy(step + 1).start()

    @pl.when(step + 1 < n_steps)
    def _():
        idx_copy(step + 1).wait()
        issue(step + 1)

        @pl.when(step + 2 < n_steps)
        def _():
            idx_copy(step + 2).start()

    pltpu.make_async_copy(buf.at[slot], buf.at[slot], sem_g.at[slot]).wait()
    return slot


def _route(lgt):
    g = [lgt[i:i + 1, :] for i in range(N_GROUPS)]
    gmax = jnp.maximum(jnp.maximum(g[0], g[1]), jnp.maximum(g[2], g[3]))
    gsum = sum(jnp.exp(gi - gmax) for gi in g)
    g_val = 1.0 / gsum
    g_idx = jnp.where(g[0] == gmax, 0, jnp.where(g[1] == gmax, 1, jnp.where(g[2] == gmax, 2, 3)))
    e = []
    for k in range(EXP_PER_GROUP):
        col = lambda i, k=k: lgt[N_GROUPS + EXP_PER_GROUP * i + k:N_GROUPS + EXP_PER_GROUP * i + k + 1, :]
        e.append(jnp.where(g_idx == 0, col(0), jnp.where(g_idx == 1, col(1), jnp.where(g_idx == 2, col(2), col(3)))))
    v1 = jnp.maximum(jnp.maximum(e[0], e[1]), jnp.maximum(e[2], e[3]))
    i1 = jnp.where(e[0] == v1, 0, jnp.where(e[1] == v1, 1, jnp.where(e[2] == v1, 2, 3)))
    neg = jnp.float32(-jnp.inf)
    r = [jnp.where(i1 == k, neg, e[k]) for k in range(EXP_PER_GROUP)]
    v2 = jnp.maximum(jnp.maximum(r[0], r[1]), jnp.maximum(r[2], r[3]))
    i2 = jnp.where((r[0] == v2) & (i1 != 0), 0,
                   jnp.where((r[1] == v2) & (i1 != 1), 1, jnp.where((r[2] == v2) & (i1 != 2), 2, 3)))
    ex = jnp.exp(v2 - v1)
    w1 = g_val / (1.0 + ex)
    w2 = g_val * ex / (1.0 + ex)
    lo = jnp.minimum(i1, i2)
    hi = jnp.maximum(i1, i2)
    w_lo = jnp.where(i1 < i2, w1, w2)
    w_hi = jnp.where(i1 < i2, w2, w1)
    pair = jnp.where(lo == 0, hi - 1, jnp.where(lo == 1, hi + 1, 5))
    bucket = (g_idx * N_PAIRS + pair).astype(_F32)
    row = lax.broadcasted_iota(jnp.int32, (SUBLANES, lgt.shape[1]), 0)
    return jnp.where(row == 0, bucket, jnp.where(row == 1, w_lo, jnp.where(row == 2, w_hi, 0.0)))


def _mixer_kernel(*refs, rows, chunk, gathered, emit_state):
    refs = list(refs)
    take = lambda n: [refs.pop(0) for _ in range(n)]
    if gathered:
        idx_hbm, src_hbm = take(2)
    else:
        h_ref, = take(1)
    (nmix_ref, wmain_ref, wif_ref, bif_ref, convw_ref, gain_ref, wout_ref, nffn_ref, wrhi_ref, wrlo_ref,
     br_ref, c0_ref, n0_ref, m0_ref, u0_ref) = take(15)
    out_ref, info_ref = take(2)
    if emit_state:
        cN_ref, nN_ref, mN_ref, uN_ref = take(4)
    c_scr, n_scr, m_scr, ubuf, ybuf = take(5)
    t = pl.program_id(1)

    @pl.when(t == 0)
    def _():
        c_scr[...] = c0_ref[...]
        n_scr[...] = n0_ref[...]
        m_scr[...] = m0_ref[...]
        ubuf[0:SUBLANES, :] = u0_ref[...]

    if gathered:
        idx_smem, hbuf, sem_i, sem_g = take(4)
        n_steps = pl.num_programs(0) * pl.num_programs(1)
        slot = _gather_step(pl.program_id(0) * pl.num_programs(1) + t, n_steps,
                            idx_hbm, src_hbm, idx_smem, hbuf, sem_i, sem_g, rows)
        h = hbuf[slot].reshape(rows, D_MODEL)
    else:
        h = h_ref[...]
    xn = _rms(h, nmix_ref[...]).astype(_BF16)

    def proj(off, width):
        return _dot(xn, wmain_ref[:, off:off + width])

    u = proj(_OFF_CC, D_MODEL) * proj(_OFF_CX, D_MODEL)
    ubuf[SUBLANES:SUBLANES + rows, :] = u
    conv = (convw_ref[0:1, :] * ubuf[SUBLANES - 2:SUBLANES - 2 + rows, :]
            + convw_ref[1:2, :] * ubuf[SUBLANES - 1:SUBLANES - 1 + rows, :]
            + convw_ref[2:3, :] * u)
    ubuf[0:SUBLANES, :] = ubuf[rows:rows + SUBLANES, :]
    ybuf[...] = jax.nn.sigmoid(proj(_OFF_GC, D_MODEL)) * (proj(_OFF_CB, D_MODEL) * conv)

    gates = _dot(xn, wif_ref[...]) + bif_ref[...]
    lane = lax.broadcasted_iota(jnp.int32, gates.shape, 1)
    is_f = (lane >= M_HEADS) & (lane < 2 * M_HEADS)
    logf = jnp.where(is_f, jax.nn.log_sigmoid(gates), 0.0)
    ri = lax.broadcasted_iota(jnp.int32, (rows, rows), 0)
    ci = lax.broadcasted_iota(jnp.int32, (rows, rows), 1)
    shift = chunk.bit_length() - 1
    tri = (((ri >> shift) == (ci >> shift)) & (ci <= ri)).astype(_BF16)
    f_hi = logf.astype(_BF16)
    f_r = logf - f_hi.astype(_F32)
    f_mid = f_r.astype(_BF16)
    f_lo = (f_r - f_mid.astype(_F32)).astype(_BF16)
    bcum = _dot(tri, f_hi) + _dot(tri, f_mid) + _dot(tri, f_lo)
    gates_t = gates.T
    bcum_t = bcum.T

    q_all = (proj(_OFF_Q, QK_W) * (M_DQK ** -0.5)).astype(_BF16)
    k_all = proj(_OFF_K, QK_W)
    v_all = proj(_OFF_V, V_W).astype(_BF16)
    gm_all = proj(_OFF_GM, V_W)

    ti = lax.broadcasted_iota(jnp.int32, (chunk, chunk), 0)
    si = lax.broadcasted_iota(jnp.int32, (chunk, chunk), 1)
    causal = si <= ti

    for c in range(rows // chunk):
        r0 = c * chunk
        for hd in range(M_HEADS):
            q = q_all[r0:r0 + chunk, hd * M_DQK:(hd + 1) * M_DQK]
            kf = k_all[r0:r0 + chunk, hd * M_DQK:(hd + 1) * M_DQK]
            v = v_all[r0:r0 + chunk, hd * M_DV:(hd + 1) * M_DV]
            b_c = bcum[r0:r0 + chunk, M_HEADS + hd:M_HEADS + hd + 1]
            i_c = gates[r0:r0 + chunk, hd:hd + 1]
            b_r = bcum_t[M_HEADS + hd:M_HEADS + hd + 1, r0:r0 + chunk]
            i_r = gates_t[hd:hd + 1, r0:r0 + chunk]
            m_prev = m_scr[hd:hd + 1, 0:1]
            c_prev = c_scr[hd]
            n_prev = n_scr[hd:hd + 1, :]

            log_d = jnp.where(causal, b_c - b_r + i_r, -jnp.inf)
            inter = b_c + m_prev
            m_t = jnp.maximum(inter, jnp.max(log_d, axis=-1, keepdims=True))
            dmat = jnp.exp(log_d - m_t)
            s = lax.dot_general(q, kf.astype(_BF16), (((1,), (1,)), ((), ())),
                                preferred_element_type=_F32) * dmat
            e_inter = jnp.exp(inter - m_t)
            num = _dot(s.astype(_BF16), v) + e_inter * _dot(q, c_prev.astype(_BF16))
            den = (jnp.sum(s, axis=-1, keepdims=True)
                   + e_inter * jnp.sum(q.astype(_F32) * n_prev, axis=-1, keepdims=True))
            hc = num / jnp.maximum(jnp.abs(den), jnp.exp(-m_t))

            b_last = b_c[chunk - 1:chunk, :]
            log_w = b_last - b_c + i_c
            m_new = jnp.maximum(b_last + m_prev, jnp.max(log_w, axis=0, keepdims=True))
            kw = kf * jnp.exp(log_w - m_new)
            decay = jnp.exp(b_last + m_prev - m_new)
            c_scr[hd] = decay * c_prev + lax.dot_general(
                kw.astype(_BF16), v, (((0,), (0,)), ((), ())), preferred_element_type=_F32)
            n_scr[hd:hd + 1, :] = decay * n_prev + jnp.sum(kw, axis=0, keepdims=True)
            m_scr[hd:hd + 1, :] = jnp.broadcast_to(m_new, (1, LANES))

            hn = hc * lax.rsqrt(jnp.mean(hc * hc, axis=-1, keepdims=True) + EPS)
            cols = slice(hd * M_DV, (hd + 1) * M_DV)
            ybuf[r0:r0 + chunk, cols] += (jax.nn.sigmoid(gm_all[r0:r0 + chunk, cols])
                                          * (hn * gain_ref[:, cols]))

    h1 = h + _dot(ybuf[...].astype(_BF16), wout_ref[...])

    x_hi, x_lo = _split2(_rms(h1, nffn_ref[...]))
    lg = (_dot(x_hi, wrhi_ref[...]) + _dot(x_lo, wrhi_ref[...]) + _dot(x_hi, wrlo_ref[...])) + br_ref[...]
    info_t = _route(lg.T)
    info_ref[0] = info_t
    out_ref[:, 0, 0:D_MODEL] = h1
    out_ref[:, 0, D_MODEL:ROW_W] = jnp.concatenate(
        [info_t, jnp.zeros((LANES - SUBLANES, rows), _F32)], axis=0).T

    if emit_state:
        @pl.when(t == pl.num_programs(1) - 1)
        def _():
            cN_ref[...] = c_scr[...]
            nN_ref[...] = n_scr[...]
            mN_ref[...] = m_scr[...]
            uN_ref[...] = ubuf[0:SUBLANES, :]


def _mixer(h, seqs, seq_len, lw, state, pos=None, emit_state=False):
    rows = min(MIX_ROWS, seq_len)
    chunk = min(MIX_CHUNK, rows)
    steps = seq_len // rows
    gathered = pos is not None
    const = lambda shape: pl.BlockSpec(shape, lambda b, t: (0,) * len(shape), pipeline_mode=pl.Buffered(1))
    any_spec = pl.BlockSpec(memory_space=pl.ANY)
    state_shapes = [(M_HEADS, M_DQK, M_DV), (SUBLANES, M_DQK), (SUBLANES, LANES), (SUBLANES, D_MODEL)]
    if gathered:
        idx, idx_rows = _idx_tiles(pos, rows)
        in_specs, args = [any_spec, any_spec], [idx, h]
    else:
        in_specs, args = [pl.BlockSpec((rows, D_MODEL), lambda b, t: (b * steps + t, 0))], [h]
    in_specs += [
        const((1, D_MODEL)), const((D_MODEL, MAIN_W)), const((D_MODEL, LANES)), const((1, LANES)),
        const((3, D_MODEL)), const((1, V_W)), const((D_MODEL, D_MODEL)), const((1, D_MODEL)),
        const((D_MODEL, LANES)), const((D_MODEL, LANES)), const((1, LANES)),
    ] + [const(s) for s in state_shapes]
    out_specs = [pl.BlockSpec((rows, 1, ROW_W), lambda b, t: (b * steps + t, 0, 0)),
                 pl.BlockSpec((1, SUBLANES, rows), lambda b, t: (b * steps + t, 0, 0))]
    out_shape = [jax.ShapeDtypeStruct((seqs * seq_len, 1, ROW_W), _F32),
                 jax.ShapeDtypeStruct((seqs * steps, SUBLANES, rows), _F32)]
    if emit_state:
        out_specs += [pl.BlockSpec(s, lambda b, t, n=len(s): (0,) * n) for s in state_shapes]
        out_shape += [jax.ShapeDtypeStruct(s, _F32) for s in state_shapes]
    scratch = [
        pltpu.VMEM((M_HEADS, M_DQK, M_DV), _F32),
        pltpu.VMEM((SUBLANES, M_DQK), _F32),
        pltpu.VMEM((SUBLANES, LANES), _F32),
        pltpu.VMEM((rows + SUBLANES, D_MODEL), _F32),
        pltpu.VMEM((rows, D_MODEL), _F32),
    ]
    if gathered:
        scratch += [pltpu.SMEM((2, idx_rows), jnp.int32), _row_buffer(rows, D_MODEL),
                    pltpu.SemaphoreType.DMA((2,)), pltpu.SemaphoreType.DMA((2,))]
    outs = pl.pallas_call(
        functools.partial(_mixer_kernel, rows=rows, chunk=chunk, gathered=gathered, emit_state=emit_state),
        grid=(seqs, steps),
        in_specs=in_specs,
        out_specs=out_specs,
        out_shape=out_shape,
        scratch_shapes=scratch,
        compiler_params=pltpu.CompilerParams(
            dimension_semantics=("arbitrary", "arbitrary"), vmem_limit_bytes=VMEM_LIMIT),
        name="mixer",
    )(*args, lw["norm_mix"], lw["w_main"], lw["w_if"], lw["b_if"], lw["conv_w"], lw["mh_gain"], lw["w_out"],
      lw["norm_ffn"], lw["w_route_hi"], lw["w_route_lo"], lw["b_route"], *state)
    return outs[0], outs[1], tuple(outs[2:])


def _expert_kernel(elo_ref, ehi_ref, nused_ref, inv_hbm, hx_hbm, nffn_ref,
                   wg_lo, wg_hi, wu_lo, wu_hi, wd_lo, wd_hi, out_ref,
                   idx_smem, xbuf, sem_i, sem_g, *, tile):
    j = pl.program_id(0)
    n_used = nused_ref[0]

    @pl.when(j < n_used)
    def _():
        slot = _gather_step(j, n_used, inv_hbm, hx_hbm, idx_smem, xbuf, sem_i, sem_g, tile)
        x = xbuf[slot].reshape(tile, ROW_W)
        hrow = x[:, 0:D_MODEL]
        w_lo = x[:, D_MODEL + 1:D_MODEL + 2]
        w_hi = x[:, D_MODEL + 2:D_MODEL + 3]
        xn = _rms(hrow, nffn_ref[...]).astype(_BF16)
        hid_lo = jax.nn.silu(_dot(xn, wg_lo[0])) * _dot(xn, wu_lo[0]) * w_lo
        hid_hi = jax.nn.silu(_dot(xn, wg_hi[0])) * _dot(xn, wu_hi[0]) * w_hi
        out_ref[:, 0, :] = hrow + _dot(hid_lo.astype(_BF16), wd_lo[0]) + _dot(hid_hi.astype(_BF16), wd_hi[0])

    @pl.when(j >= n_used)
    def _():
        out_ref[...] = jnp.zeros_like(out_ref)


_PAIR_LO = np.array([0, 0, 0, 1, 1, 2], np.int32)
_PAIR_HI = np.array([1, 2, 3, 2, 3, 3], np.int32)


def _sort_plan(bucket, n_tokens, tile):
    n_tiles = -(-n_tokens // tile) + N_BUCKETS
    ids = jnp.arange(N_BUCKETS, dtype=jnp.int32)
    order = jnp.argsort(bucket, stable=True).astype(jnp.int32)
    rank = jnp.argsort(order).astype(jnp.int32)
    counts = jnp.sum((bucket[:, None] == ids[None, :]).astype(jnp.int32), axis=0)
    tiles = (counts + tile - 1) // tile
    tile_end = jnp.cumsum(tiles)
    tile_start = tile_end - tiles
    cnt_start = jnp.cumsum(counts) - counts
    n_used = tile_end[-1]
    shift = tile_start * tile - cnt_start
    pos = rank + jnp.sum(jnp.where(bucket[:, None] == ids[None, :], shift[None, :], 0), axis=1)
    j = jnp.arange(n_tiles, dtype=jnp.int32)
    last = jnp.maximum(n_used - 1, 0)
    jj = jnp.minimum(j, last)
    tb = jnp.sum((jj[:, None] >= tile_end[None, :]).astype(jnp.int32), axis=1)
    r = jnp.arange(n_tiles * tile, dtype=jnp.int32)
    rb = jnp.repeat(tb, tile)
    rsel = rb[:, None] == ids[None, :]
    slot = r - jnp.sum(jnp.where(rsel, shift[None, :], 0), axis=1)
    slot_end = jnp.sum(jnp.where(rsel, (cnt_start + counts)[None, :], 0), axis=1)
    valid = (slot < slot_end) & (jnp.repeat(j, tile) < n_used)
    inv = jnp.where(valid, order[jnp.clip(slot, 0, n_tokens - 1)], r % n_tokens).astype(jnp.int32)
    grp = tb // N_PAIRS
    pr = tb % N_PAIRS
    e_lo = (grp * EXP_PER_GROUP + jnp.asarray(_PAIR_LO)[pr]).astype(jnp.int32)
    e_hi = (grp * EXP_PER_GROUP + jnp.asarray(_PAIR_HI)[pr]).astype(jnp.int32)
    return e_lo, e_hi, n_used.reshape(1).astype(jnp.int32), inv, pos.astype(jnp.int32), n_tiles


def _experts(hx, info, lw):
    n_tokens = hx.shape[0]
    tile = min(EXPERT_ROWS, n_tokens)
    bucket = info[:, 0, :].reshape(n_tokens).astype(jnp.int32)
    e_lo, e_hi, n_used, inv, pos, n_tiles = _sort_plan(bucket, n_tokens, tile)
    inv, idx_rows = _idx_tiles(inv, tile)
    any_spec = pl.BlockSpec(memory_space=pl.ANY)
    vec = pl.BlockSpec((1, D_MODEL), lambda j, lo, hi, nu: (0, 0))
    w_in_lo = pl.BlockSpec((1, D_MODEL, D_EXPERT), lambda j, lo, hi, nu: (lo[j], 0, 0))
    w_in_hi = pl.BlockSpec((1, D_MODEL, D_EXPERT), lambda j, lo, hi, nu: (hi[j], 0, 0))
    w_dn_lo = pl.BlockSpec((1, D_EXPERT, D_MODEL), lambda j, lo, hi, nu: (lo[j], 0, 0))
    w_dn_hi = pl.BlockSpec((1, D_EXPERT, D_MODEL), lambda j, lo, hi, nu: (hi[j], 0, 0))
    ys = pl.pallas_call(
        functools.partial(_expert_kernel, tile=tile),
        grid_spec=pltpu.PrefetchScalarGridSpec(
            num_scalar_prefetch=3,
            grid=(n_tiles,),
            in_specs=[any_spec, any_spec, vec, w_in_lo, w_in_hi, w_in_lo, w_in_hi, w_dn_lo, w_dn_hi],
            out_specs=pl.BlockSpec((tile, 1, D_MODEL), lambda j, lo, hi, nu: (j, 0, 0)),
            scratch_shapes=[
                pltpu.SMEM((2, idx_rows), jnp.int32),
                _row_buffer(tile, ROW_W),
                pltpu.SemaphoreType.DMA((2,)),
                pltpu.SemaphoreType.DMA((2,)),
            ],
        ),
        out_shape=jax.ShapeDtypeStruct((n_tiles * tile, 1, D_MODEL), _F32),
        compiler_params=pltpu.CompilerParams(
            dimension_semantics=("arbitrary",), vmem_limit_bytes=VMEM_LIMIT),
        name="experts",
    )(e_lo, e_hi, n_used, inv, hx, lw["norm_ffn"],
      lw["w_gate"], lw["w_gate"], lw["w_up"], lw["w_up"], lw["w_down"], lw["w_down"])
    return ys, pos


def _final_norm_kernel(pos_hbm, ys_hbm, g_ref, out_ref, idx_smem, buf, sem_i, sem_g, *, rows):
    slot = _gather_step(pl.program_id(0), pl.num_programs(0), pos_hbm, ys_hbm, idx_smem, buf, sem_i, sem_g, rows)
    out_ref[...] = _rms(buf[slot].reshape(rows, D_MODEL), g_ref[...])


def _final_norm(ys, pos, g):
    n_tokens = pos.shape[0]
    rows = NORM_ROWS
    pos, idx_rows = _idx_tiles(pos, rows)
    any_spec = pl.BlockSpec(memory_space=pl.ANY)
    return pl.pallas_call(
        functools.partial(_final_norm_kernel, rows=rows),
        grid=(n_tokens // rows,),
        in_specs=[any_spec, any_spec, pl.BlockSpec((1, D_MODEL), lambda i: (0, 0))],
        out_specs=pl.BlockSpec((rows, D_MODEL), lambda i: (i, 0)),
        out_shape=jax.ShapeDtypeStruct((n_tokens, D_MODEL), _F32),
        scratch_shapes=[pltpu.SMEM((2, idx_rows), jnp.int32), _row_buffer(rows, D_MODEL),
                        pltpu.SemaphoreType.DMA((2,)), pltpu.SemaphoreType.DMA((2,))],
        compiler_params=pltpu.CompilerParams(dimension_semantics=("arbitrary",)),
        name="final_norm",
    )(pos, ys, g)


def _layer_weights(l, norm_mix, w_in, b_if, conv_w, mh_gain, w_out, norm_ffn,
                   w_group, b_group, w_router, b_router, w_gate, w_up, w_down):
    w = w_in[l]
    o_if = 2 * QK_W + V_W
    o_rest = o_if + 2 * M_HEADS
    w_main = jnp.concatenate([w[:, :o_if], w[:, o_rest:]], axis=1).astype(_BF16)
    w_if = jnp.pad(w[:, o_if:o_rest], ((0, 0), (0, LANES - 2 * M_HEADS))).astype(_BF16)
    pad_lanes = lambda a: jnp.pad(a, ((0, 0), (0, LANES - a.shape[1])))
    w_route = pad_lanes(jnp.concatenate([w_group[l], w_router[l]], axis=1))
    w_route_hi = w_route.astype(_BF16)
    return {
        "norm_mix": norm_mix[l][None], "w_main": w_main, "w_if": w_if,
        "b_if": pad_lanes(b_if[l][None]), "conv_w": conv_w[l], "mh_gain": mh_gain[l][None],
        "w_out": w_out[l].astype(_BF16), "norm_ffn": norm_ffn[l][None],
        "w_route_hi": w_route_hi, "w_route_lo": (w_route - w_route_hi.astype(_F32)).astype(_BF16),
        "b_route": pad_lanes(jnp.concatenate([b_group[l], b_router[l]])[None]),
        "w_gate": w_gate[l].astype(_BF16), "w_up": w_up[l].astype(_BF16), "w_down": w_down[l].astype(_BF16),
    }


def kernel(x, meta_tokens, norm_mix, w_in, b_if, conv_w, mh_gain, w_out, norm_ffn, w_group, b_group,
           w_router, b_router, w_gate, w_up, w_down, norm_final):
    batch, seq, d = x.shape
    depth = w_in.shape[0]
    assert d == D_MODEL and seq % MIX_ROWS == 0 and meta_tokens.shape == (N_META, D_MODEL)
    assert (batch * seq) % NORM_ROWS == 0
    layers = [_layer_weights(l, norm_mix, w_in, b_if, conv_w, mh_gain, w_out, norm_ffn,
                             w_group, b_group, w_router, b_router, w_gate, w_up, w_down)
              for l in range(depth)]
    zero_state = (jnp.zeros((M_HEADS, M_DQK, M_DV), _F32), jnp.zeros((SUBLANES, M_DQK), _F32),
                  jnp.zeros((SUBLANES, LANES), _F32), jnp.zeros((SUBLANES, D_MODEL), _F32))

    hm = jnp.concatenate([jnp.zeros((REF_CHUNK - N_META, D_MODEL), _F32), meta_tokens.astype(_F32)], axis=0)
    pos = None
    states = []
    for l in range(depth):
        hmx, info, st = _mixer(hm, 1, REF_CHUNK, layers[l], zero_state, pos, emit_state=True)
        states.append(st)
        if l + 1 < depth:
            hm, pos = _experts(hmx, info, layers[l])

    h = x.reshape(batch * seq, D_MODEL)
    pos = None
    for l in range(depth):
        hx, info, _ = _mixer(h, batch, seq, layers[l], states[l], pos)
        h, pos = _experts(hx, info, layers[l])
    return _final_norm(h, pos, norm_final[None]).reshape(batch, seq, D_MODEL)
```

```python
import functools

import jax
import jax.numpy as jnp
import numpy as np
from jax import lax
from jax.experimental import pallas as pl
from jax.experimental.pallas import tpu as pltpu

D_MODEL = 1024
N_META = 16
M_HEADS = 4
M_DQK = 128
M_DV = 256
REF_CHUNK = 64
N_GROUPS = 4
EXP_PER_GROUP = 4
N_PAIRS = 6
N_BUCKETS = N_GROUPS * N_PAIRS
D_EXPERT = 256
EPS = 1e-6
QK_W = M_HEADS * M_DQK
V_W = M_HEADS * M_DV

LANES = 128
SUBLANES = 8
ROW_W = D_MODEL + LANES
VMEM_LIMIT = 56 * 1024 * 1024

_OFF_Q = 0
_OFF_K = _OFF_Q + QK_W
_OFF_V = _OFF_K + QK_W
_OFF_GM = _OFF_V + V_W
_OFF_CX = _OFF_GM + V_W
_OFF_CB = _OFF_CX + D_MODEL
_OFF_CC = _OFF_CB + D_MODEL
_OFF_GC = _OFF_CC + D_MODEL
MAIN_W = _OFF_GC + D_MODEL

MIX_CHUNK = 256
MIX_ROWS = 256
EXPERT_ROWS = 256
NORM_ROWS = 256

_F32 = jnp.float32
_BF16 = jnp.bfloat16


def _rms(x, g):
    return x * lax.rsqrt(jnp.mean(x * x, axis=-1, keepdims=True) + EPS) * g


def _dot(a, b):
    return jnp.dot(a, b, preferred_element_type=_F32)


def _split2(x):
    hi = x.astype(_BF16)
    return hi, (x - hi.astype(_F32)).astype(_BF16)


MIN_IDX_TILE = 128


LOOKAHEAD_TILES = 2


def _idx_tiles(idx, rows):
    idx_rows = max(rows, MIN_IDX_TILE)
    tiles = jnp.pad(idx.reshape(-1, rows), ((0, 0), (0, idx_rows - rows)))
    wrap = tiles[jnp.arange(LOOKAHEAD_TILES) % tiles.shape[0]]
    return jnp.concatenate([tiles, wrap], axis=0).reshape(-1), idx_rows


def _row_buffer(rows, width):
    return pltpu.VMEM((2, rows // SUBLANES, SUBLANES, width), _F32)


class RowGather:
    def __init__(self, idx_hbm, src_hbm, idx_smem, buf, sem_i, sem_g, rows):
        self.idx_hbm, self.src_hbm, self.idx_smem, self.buf = idx_hbm, src_hbm, idx_smem, buf
        self.sem_i, self.sem_g, self.rows = sem_i, sem_g, rows
        self.idx_rows = idx_smem.shape[1]

    def _idx_copy(self, s):
        return pltpu.make_async_copy(self.idx_hbm.at[pl.ds(s * self.idx_rows, self.idx_rows)],
                                     self.idx_smem.at[s % 2], self.sem_i.at[s % 2])

    def _issue(self, s):
        for i in range(self.rows):
            pltpu.make_async_copy(self.src_hbm.at[self.idx_smem[s % 2, i]],
                                  self.buf.at[s % 2, i // SUBLANES, pl.ds(i % SUBLANES, 1)],
                                  self.sem_g.at[s % 2]).start()

    def _wait_rows(self, s):
        pltpu.make_async_copy(self.buf.at[s % 2], self.buf.at[s % 2], self.sem_g.at[s % 2]).wait()

    def begin(self, step):
        @pl.when(step == 0)
        def _():
            self._idx_copy(step).start()
            self._idx_copy(step).wait()
            self._issue(step)
            self._idx_copy(step + 1).start()

        self._wait_rows(step)
        return step % 2

    def prefetch(self, step):
        self._idx_copy(step + 1).wait()
        self._issue(step + 1)
        self._idx_copy(step + 2).start()

    def drain(self, step):
        self._wait_rows(step)
        self._idx_copy(step + 1).wait()


def _route(lgt):
    g = [lgt[i:i + 1, :] for i in range(N_GROUPS)]
    gmax = jnp.maximum(jnp.maximum(g[0], g[1]), jnp.maximum(g[2], g[3]))
    gsum = sum(jnp.exp(gi - gmax) for gi in g)
    g_val = 1.0 / gsum
    g_idx = jnp.where(g[0] == gmax, 0, jnp.where(g[1] == gmax, 1, jnp.where(g[2] == gmax, 2, 3)))
    e = []
    for k in range(EXP_PER_GROUP):
        col = lambda i, k=k: lgt[N_GROUPS + EXP_PER_GROUP * i + k:N_GROUPS + EXP_PER_GROUP * i + k + 1, :]
        e.append(jnp.where(g_idx == 0, col(0), jnp.where(g_idx == 1, col(1), jnp.where(g_idx == 2, col(2), col(3)))))
    v1 = jnp.maximum(jnp.maximum(e[0], e[1]), jnp.maximum(e[2], e[3]))
    i1 = jnp.where(e[0] == v1, 0, jnp.where(e[1] == v1, 1, jnp.where(e[2] == v1, 2, 3)))
    neg = jnp.float32(-jnp.inf)
    r = [jnp.where(i1 == k, neg, e[k]) for k in range(EXP_PER_GROUP)]
    v2 = jnp.maximum(jnp.maximum(r[0], r[1]), jnp.maximum(r[2], r[3]))
    i2 = jnp.where((r[0] == v2) & (i1 != 0), 0,
                   jnp.where((r[1] == v2) & (i1 != 1), 1, jnp.where((r[2] == v2) & (i1 != 2), 2, 3)))
    ex = jnp.exp(v2 - v1)
    w1 = g_val / (1.0 + ex)
    w2 = g_val * ex / (1.0 + ex)
    lo = jnp.minimum(i1, i2)
    hi = jnp.maximum(i1, i2)
    w_lo = jnp.where(i1 < i2, w1, w2)
    w_hi = jnp.where(i1 < i2, w2, w1)
    pair = jnp.where(lo == 0, hi - 1, jnp.where(lo == 1, hi + 1, 5))
    bucket = (g_idx * N_PAIRS + pair).astype(_F32)
    row = lax.broadcasted_iota(jnp.int32, (SUBLANES, lgt.shape[1]), 0)
    return jnp.where(row == 0, bucket, jnp.where(row == 1, w_lo, jnp.where(row == 2, w_hi, 0.0)))


def _mixer_kernel(*refs, rows, chunk, gathered, emit_state):
    refs = list(refs)
    take = lambda n: [refs.pop(0) for _ in range(n)]
    if gathered:
        idx_hbm, src_hbm = take(2)
    else:
        h_ref, = take(1)
    (nmix_ref, wmain_ref, wif_ref, bif_ref, convw_ref, gain_ref, wout_ref, nffn_ref, wrhi_ref, wrlo_ref,
     br_ref, c0_ref, n0_ref, m0_ref, u0_ref) = take(15)
    out_ref, info_ref = take(2)
    if emit_state:
        cN_ref, nN_ref, mN_ref, uN_ref = take(4)
    c_scr, n_scr, m_scr, ubuf, ybuf = take(5)
    t = pl.program_id(1)

    @pl.when(t == 0)
    def _():
        c_scr[...] = c0_ref[...]
        n_scr[...] = n0_ref[...]
        m_scr[...] = m0_ref[...]
        ubuf[0:SUBLANES, :] = u0_ref[...]

    if gathered:
        idx_smem, hbuf, sem_i, sem_g = take(4)
        gather = RowGather(idx_hbm, src_hbm, idx_smem, hbuf, sem_i, sem_g, rows)
        step = pl.program_id(0) * pl.num_programs(1) + t
        slot = gather.begin(step)
        h = hbuf[slot].reshape(rows, D_MODEL)
        gather.prefetch(step)
    else:
        h = h_ref[...]
    xn = _rms(h, nmix_ref[...]).astype(_BF16)

    def proj(off, width):
        return _dot(xn, wmain_ref[:, off:off + width])

    u = proj(_OFF_CC, D_MODEL) * proj(_OFF_CX, D_MODEL)
    ubuf[SUBLANES:SUBLANES + rows, :] = u
    conv = (convw_ref[0:1, :] * ubuf[SUBLANES - 2:SUBLANES - 2 + rows, :]
            + convw_ref[1:2, :] * ubuf[SUBLANES - 1:SUBLANES - 1 + rows, :]
            + convw_ref[2:3, :] * u)
    ubuf[0:SUBLANES, :] = ubuf[rows:rows + SUBLANES, :]
    ybuf[...] = jax.nn.sigmoid(proj(_OFF_GC, D_MODEL)) * (proj(_OFF_CB, D_MODEL) * conv)

    gates = _dot(xn, wif_ref[...]) + bif_ref[...]
    lane = lax.broadcasted_iota(jnp.int32, gates.shape, 1)
    is_f = (lane >= M_HEADS) & (lane < 2 * M_HEADS)
    logf = jnp.where(is_f, jax.nn.log_sigmoid(gates), 0.0)
    ri = lax.broadcasted_iota(jnp.int32, (rows, rows), 0)
    ci = lax.broadcasted_iota(jnp.int32, (rows, rows), 1)
    shift = chunk.bit_length() - 1
    tri = (((ri >> shift) == (ci >> shift)) & (ci <= ri)).astype(_BF16)
    f_hi = logf.astype(_BF16)
    f_r = logf - f_hi.astype(_F32)
    f_mid = f_r.astype(_BF16)
    f_lo = (f_r - f_mid.astype(_F32)).astype(_BF16)
    bcum = _dot(tri, f_hi) + _dot(tri, f_mid) + _dot(tri, f_lo)
    gates_t = gates.T
    bcum_t = bcum.T

    q_all = (proj(_OFF_Q, QK_W) * (M_DQK ** -0.5)).astype(_BF16)
    k_all = proj(_OFF_K, QK_W)
    v_all = proj(_OFF_V, V_W).astype(_BF16)
    gm_all = proj(_OFF_GM, V_W)

    ti = lax.broadcasted_iota(jnp.int32, (chunk, chunk), 0)
    si = lax.broadcasted_iota(jnp.int32, (chunk, chunk), 1)
    causal = si <= ti

    for c in range(rows // chunk):
        r0 = c * chunk
        for hd in range(M_HEADS):
            q = q_all[r0:r0 + chunk, hd * M_DQK:(hd + 1) * M_DQK]
            kf = k_all[r0:r0 + chunk, hd * M_DQK:(hd + 1) * M_DQK]
            v = v_all[r0:r0 + chunk, hd * M_DV:(hd + 1) * M_DV]
            b_c = bcum[r0:r0 + chunk, M_HEADS + hd:M_HEADS + hd + 1]
            i_c = gates[r0:r0 + chunk, hd:hd + 1]
            b_r = bcum_t[M_HEADS + hd:M_HEADS + hd + 1, r0:r0 + chunk]
            i_r = gates_t[hd:hd + 1, r0:r0 + chunk]
            m_prev = m_scr[hd:hd + 1, 0:1]
            c_prev = c_scr[hd]
            n_prev = n_scr[hd:hd + 1, :]

            log_d = jnp.where(causal, b_c - b_r + i_r, -jnp.inf)
            inter = b_c + m_prev
            m_t = jnp.maximum(inter, jnp.max(log_d, axis=-1, keepdims=True))
            dmat = jnp.exp(log_d - m_t)
            s = lax.dot_general(q, kf.astype(_BF16), (((1,), (1,)), ((), ())),
                                preferred_element_type=_F32) * dmat
            e_inter = jnp.exp(inter - m_t)
            num = _dot(s.astype(_BF16), v) + e_inter * _dot(q, c_prev.astype(_BF16))
            den = (jnp.sum(s, axis=-1, keepdims=True)
                   + e_inter * jnp.sum(q.astype(_F32) * n_prev, axis=-1, keepdims=True))
            hc = num / jnp.maximum(jnp.abs(den), jnp.exp(-m_t))

            b_last = b_c[chunk - 1:chunk, :]
            log_w = b_last - b_c + i_c
            m_new = jnp.maximum(b_last + m_prev, jnp.max(log_w, axis=0, keepdims=True))
            kw = kf * jnp.exp(log_w - m_new)
            decay = jnp.exp(b_last + m_prev - m_new)
            c_scr[hd] = decay * c_prev + lax.dot_general(
                kw.astype(_BF16), v, (((0,), (0,)), ((), ())), preferred_element_type=_F32)
            n_scr[hd:hd + 1, :] = decay * n_prev + jnp.sum(kw, axis=0, keepdims=True)
            m_scr[hd:hd + 1, :] = jnp.broadcast_to(m_new, (1, LANES))

            hn = hc * lax.rsqrt(jnp.mean(hc * hc, axis=-1, keepdims=True) + EPS)
            cols = slice(hd * M_DV, (hd + 1) * M_DV)
            ybuf[r0:r0 + chunk, cols] += (jax.nn.sigmoid(gm_all[r0:r0 + chunk, cols])
                                          * (hn * gain_ref[:, cols]))

    h1 = h + _dot(ybuf[...].astype(_BF16), wout_ref[...])

    x_hi, x_lo = _split2(_rms(h1, nffn_ref[...]))
    lg = (_dot(x_hi, wrhi_ref[...]) + _dot(x_lo, wrhi_ref[...]) + _dot(x_hi, wrlo_ref[...])) + br_ref[...]
    info_t = _route(lg.T)
    info_ref[0] = info_t
    out_ref[:, 0, 0:D_MODEL] = h1
    out_ref[:, 0, D_MODEL:ROW_W] = jnp.concatenate(
        [info_t, jnp.zeros((LANES - SUBLANES, rows), _F32)], axis=0).T

    if gathered:
        @pl.when(step == pl.num_programs(0) * pl.num_programs(1) - 1)
        def _():
            gather.drain(step + 1)

    if emit_state:
        @pl.when(t == pl.num_programs(1) - 1)
        def _():
            cN_ref[...] = c_scr[...]
            nN_ref[...] = n_scr[...]
            mN_ref[...] = m_scr[...]
            uN_ref[...] = ubuf[0:SUBLANES, :]


def _mixer(h, seqs, seq_len, lw, state, pos=None, emit_state=False):
    rows = min(MIX_ROWS, seq_len)
    chunk = min(MIX_CHUNK, rows)
    steps = seq_len // rows
    gathered = pos is not None
    const = lambda shape: pl.BlockSpec(shape, lambda b, t: (0,) * len(shape), pipeline_mode=pl.Buffered(1))
    any_spec = pl.BlockSpec(memory_space=pl.ANY)
    state_shapes = [(M_HEADS, M_DQK, M_DV), (SUBLANES, M_DQK), (SUBLANES, LANES), (SUBLANES, D_MODEL)]
    if gathered:
        idx, idx_rows = _idx_tiles(pos, rows)
        in_specs, args = [any_spec, any_spec], [idx, h]
    else:
        in_specs, args = [pl.BlockSpec((rows, D_MODEL), lambda b, t: (b * steps + t, 0))], [h]
    in_specs += [
        const((1, D_MODEL)), const((D_MODEL, MAIN_W)), const((D_MODEL, LANES)), const((1, LANES)),
        const((3, D_MODEL)), const((1, V_W)), const((D_MODEL, D_MODEL)), const((1, D_MODEL)),
        const((D_MODEL, LANES)), const((D_MODEL, LANES)), const((1, LANES)),
    ] + [const(s) for s in state_shapes]
    out_specs = [pl.BlockSpec((rows, 1, ROW_W), lambda b, t: (b * steps + t, 0, 0)),
                 pl.BlockSpec((1, SUBLANES, rows), lambda b, t: (b * steps + t, 0, 0))]
    out_shape = [jax.ShapeDtypeStruct((seqs * seq_len, 1, ROW_W), _F32),
                 jax.ShapeDtypeStruct((seqs * steps, SUBLANES, rows), _F32)]
    if emit_state:
        out_specs += [pl.BlockSpec(s, lambda b, t, n=len(s): (0,) * n) for s in state_shapes]
        out_shape += [jax.ShapeDtypeStruct(s, _F32) for s in state_shapes]
    scratch = [
        pltpu.VMEM((M_HEADS, M_DQK, M_DV), _F32),
        pltpu.VMEM((SUBLANES, M_DQK), _F32),
        pltpu.VMEM((SUBLANES, LANES), _F32),
        pltpu.VMEM((rows + SUBLANES, D_MODEL), _F32),
        pltpu.VMEM((rows, D_MODEL), _F32),
    ]
    if gathered:
        scratch += [pltpu.SMEM((2, idx_rows), jnp.int32), _row_buffer(rows, D_MODEL),
                    pltpu.SemaphoreType.DMA((2,)), pltpu.SemaphoreType.DMA((2,))]
    outs = pl.pallas_call(
        functools.partial(_mixer_kernel, rows=rows, chunk=chunk, gathered=gathered, emit_state=emit_state),
        grid=(seqs, steps),
        in_specs=in_specs,
        out_specs=out_specs,
        out_shape=out_shape,
        scratch_shapes=scratch,
        compiler_params=pltpu.CompilerParams(
            dimension_semantics=("arbitrary", "arbitrary"), vmem_limit_bytes=VMEM_LIMIT),
        name="mixer",
    )(*args, lw["norm_mix"], lw["w_main"], lw["w_if"], lw["b_if"], lw["conv_w"], lw["mh_gain"], lw["w_out"],
      lw["norm_ffn"], lw["w_route_hi"], lw["w_route_lo"], lw["b_route"], *state)
    return outs[0], outs[1], tuple(outs[2:])


def _expert_kernel(elo_ref, ehi_ref, nused_ref, inv_hbm, hx_hbm, nffn_ref,
                   wg_lo, wg_hi, wu_lo, wu_hi, wd_lo, wd_hi, out_ref,
                   idx_smem, xbuf, sem_i, sem_g, *, tile):
    j = pl.program_id(0)
    n_used = nused_ref[0]
    gather = RowGather(inv_hbm, hx_hbm, idx_smem, xbuf, sem_i, sem_g, tile)

    @pl.when(j < n_used)
    def _():
        slot = gather.begin(j)
        x = xbuf[slot].reshape(tile, ROW_W)
        gather.prefetch(j)
        hrow = x[:, 0:D_MODEL]
        w_lo = x[:, D_MODEL + 1:D_MODEL + 2]
        w_hi = x[:, D_MODEL + 2:D_MODEL + 3]
        xn = _rms(hrow, nffn_ref[...]).astype(_BF16)
        hid_lo = jax.nn.silu(_dot(xn, wg_lo[0])) * _dot(xn, wu_lo[0]) * w_lo
        hid_hi = jax.nn.silu(_dot(xn, wg_hi[0])) * _dot(xn, wu_hi[0]) * w_hi
        out_ref[:, 0, :] = hrow + _dot(hid_lo.astype(_BF16), wd_lo[0]) + _dot(hid_hi.astype(_BF16), wd_hi[0])

    @pl.when(j == n_used)
    def _():
        gather.drain(j)

    @pl.when(j >= n_used)
    def _():
        out_ref[...] = jnp.zeros_like(out_ref)


_PAIR_LO = np.array([0, 0, 0, 1, 1, 2], np.int32)
_PAIR_HI = np.array([1, 2, 3, 2, 3, 3], np.int32)


def _sort_plan(bucket, n_tokens, tile):
    n_tiles = -(-n_tokens // tile) + N_BUCKETS + 1
    ids = jnp.arange(N_BUCKETS, dtype=jnp.int32)
    order = jnp.argsort(bucket, stable=True).astype(jnp.int32)
    rank = jnp.argsort(order).astype(jnp.int32)
    counts = jnp.sum((bucket[:, None] == ids[None, :]).astype(jnp.int32), axis=0)
    tiles = (counts + tile - 1) // tile
    tile_end = jnp.cumsum(tiles)
    tile_start = tile_end - tiles
    cnt_start = jnp.cumsum(counts) - counts
    n_used = tile_end[-1]
    shift = tile_start * tile - cnt_start
    pos = rank + jnp.sum(jnp.where(bucket[:, None] == ids[None, :], shift[None, :], 0), axis=1)
    j = jnp.arange(n_tiles, dtype=jnp.int32)
    last = jnp.maximum(n_used - 1, 0)
    jj = jnp.minimum(j, last)
    tb = jnp.sum((jj[:, None] >= tile_end[None, :]).astype(jnp.int32), axis=1)
    r = jnp.arange(n_tiles * tile, dtype=jnp.int32)
    rb = jnp.repeat(tb, tile)
    rsel = rb[:, None] == ids[None, :]
    slot = r - jnp.sum(jnp.where(rsel, shift[None, :], 0), axis=1)
    slot_end = jnp.sum(jnp.where(rsel, (cnt_start + counts)[None, :], 0), axis=1)
    valid = (slot < slot_end) & (jnp.repeat(j, tile) < n_used)
    inv = jnp.where(valid, order[jnp.clip(slot, 0, n_tokens - 1)], r % n_tokens).astype(jnp.int32)
    grp = tb // N_PAIRS
    pr = tb % N_PAIRS
    e_lo = (grp * EXP_PER_GROUP + jnp.asarray(_PAIR_LO)[pr]).astype(jnp.int32)
    e_hi = (grp * EXP_PER_GROUP + jnp.asarray(_PAIR_HI)[pr]).astype(jnp.int32)
    return e_lo, e_hi, n_used.reshape(1).astype(jnp.int32), inv, pos.astype(jnp.int32), n_tiles


def _experts(hx, info, lw):
    n_tokens = hx.shape[0]
    tile = min(EXPERT_ROWS, n_tokens)
    bucket = info[:, 0, :].reshape(n_tokens).astype(jnp.int32)
    e_lo, e_hi, n_used, inv, pos, n_tiles = _sort_plan(bucket, n_tokens, tile)
    inv, idx_rows = _idx_tiles(inv, tile)
    any_spec = pl.BlockSpec(memory_space=pl.ANY)
    vec = pl.BlockSpec((1, D_MODEL), lambda j, lo, hi, nu: (0, 0))
    w_in_lo = pl.BlockSpec((1, D_MODEL, D_EXPERT), lambda j, lo, hi, nu: (lo[j], 0, 0))
    w_in_hi = pl.BlockSpec((1, D_MODEL, D_EXPERT), lambda j, lo, hi, nu: (hi[j], 0, 0))
    w_dn_lo = pl.BlockSpec((1, D_EXPERT, D_MODEL), lambda j, lo, hi, nu: (lo[j], 0, 0))
    w_dn_hi = pl.BlockSpec((1, D_EXPERT, D_MODEL), lambda j, lo, hi, nu: (hi[j], 0, 0))
    ys = pl.pallas_call(
        functools.partial(_expert_kernel, tile=tile),
        grid_spec=pltpu.PrefetchScalarGridSpec(
            num_scalar_prefetch=3,
            grid=(n_tiles,),
            in_specs=[any_spec, any_spec, vec, w_in_lo, w_in_hi, w_in_lo, w_in_hi, w_dn_lo, w_dn_hi],
            out_specs=pl.BlockSpec((tile, 1, D_MODEL), lambda j, lo, hi, nu: (j, 0, 0)),
            scratch_shapes=[
                pltpu.SMEM((2, idx_rows), jnp.int32),
                _row_buffer(tile, ROW_W),
                pltpu.SemaphoreType.DMA((2,)),
                pltpu.SemaphoreType.DMA((2,)),
            ],
        ),
        out_shape=jax.ShapeDtypeStruct((n_tiles * tile, 1, D_MODEL), _F32),
        compiler_params=pltpu.CompilerParams(
            dimension_semantics=("arbitrary",), vmem_limit_bytes=VMEM_LIMIT),
        name="experts",
    )(e_lo, e_hi, n_used, inv, hx, lw["norm_ffn"],
      lw["w_gate"], lw["w_gate"], lw["w_up"], lw["w_up"], lw["w_down"], lw["w_down"])
    return ys, pos


def _final_norm_kernel(pos_hbm, ys_hbm, g_ref, out_ref, idx_smem, buf, sem_i, sem_g, *, rows):
    step = pl.program_id(0)
    gather = RowGather(pos_hbm, ys_hbm, idx_smem, buf, sem_i, sem_g, rows)
    slot = gather.begin(step)
    x = buf[slot].reshape(rows, D_MODEL)
    gather.prefetch(step)
    out_ref[...] = _rms(x, g_ref[...])

    @pl.when(step == pl.num_programs(0) - 1)
    def _():
        gather.drain(step + 1)


def _final_norm(ys, pos, g):
    n_tokens = pos.shape[0]
    rows = NORM_ROWS
    pos, idx_rows = _idx_tiles(pos, rows)
    any_spec = pl.BlockSpec(memory_space=pl.ANY)
    return pl.pallas_call(
        functools.partial(_final_norm_kernel, rows=rows),
        grid=(n_tokens // rows,),
        in_specs=[any_spec, any_spec, pl.BlockSpec((1, D_MODEL), lambda i: (0, 0))],
        out_specs=pl.BlockSpec((rows, D_MODEL), lambda i: (i, 0)),
        out_shape=jax.ShapeDtypeStruct((n_tokens, D_MODEL), _F32),
        scratch_shapes=[pltpu.SMEM((2, idx_rows), jnp.int32), _row_buffer(rows, D_MODEL),
                        pltpu.SemaphoreType.DMA((2,)), pltpu.SemaphoreType.DMA((2,))],
        compiler_params=pltpu.CompilerParams(dimension_semantics=("arbitrary",)),
        name="final_norm",
    )(pos, ys, g)


def _layer_weights(l, norm_mix, w_in, b_if, conv_w, mh_gain, w_out, norm_ffn,
                   w_group, b_group, w_router, b_router, w_gate, w_up, w_down):
    w = w_in[l]
    o_if = 2 * QK_W + V_W
    o_rest = o_if + 2 * M_HEADS
    w_main = jnp.concatenate([w[:, :o_if], w[:, o_rest:]], axis=1).astype(_BF16)
    w_if = jnp.pad(w[:, o_if:o_rest], ((0, 0), (0, LANES - 2 * M_HEADS))).astype(_BF16)
    pad_lanes = lambda a: jnp.pad(a, ((0, 0), (0, LANES - a.shape[1])))
    w_route = pad_lanes(jnp.concatenate([w_group[l], w_router[l]], axis=1))
    w_route_hi = w_route.astype(_BF16)
    return {
        "norm_mix": norm_mix[l][None], "w_main": w_main, "w_if": w_if,
        "b_if": pad_lanes(b_if[l][None]), "conv_w": conv_w[l], "mh_gain": mh_gain[l][None],
        "w_out": w_out[l].astype(_BF16), "norm_ffn": norm_ffn[l][None],
        "w_route_hi": w_route_hi, "w_route_lo": (w_route - w_route_hi.astype(_F32)).astype(_BF16),
        "b_route": pad_lanes(jnp.concatenate([b_group[l], b_router[l]])[None]),
        "w_gate": w_gate[l].astype(_BF16), "w_up": w_up[l].astype(_BF16), "w_down": w_down[l].astype(_BF16),
    }


def kernel(x, meta_tokens, norm_mix, w_in, b_if, conv_w, mh_gain, w_out, norm_ffn, w_group, b_group,
           w_router, b_router, w_gate, w_up, w_down, norm_final):
    batch, seq, d = x.shape
    depth = w_in.shape[0]
    assert d == D_MODEL and seq % MIX_ROWS == 0 and meta_tokens.shape == (N_META, D_MODEL)
    assert (batch * seq) % NORM_ROWS == 0
    layers = [_layer_weights(l, norm_mix, w_in, b_if, conv_w, mh_gain, w_out, norm_ffn,
                             w_group, b_group, w_router, b_router, w_gate, w_up, w_down)
              for l in range(depth)]
    zero_state = (jnp.zeros((M_HEADS, M_DQK, M_DV), _F32), jnp.zeros((SUBLANES, M_DQK), _F32),
                  jnp.zeros((SUBLANES, LANES), _F32), jnp.zeros((SUBLANES, D_MODEL), _F32))

    hm = jnp.concatenate([jnp.zeros((REF_CHUNK - N_META, D_MODEL), _F32), meta_tokens.astype(_F32)], axis=0)
    pos = None
    states = []
    for l in range(depth):
        hmx, info, st = _mixer(hm, 1, REF_CHUNK, layers[l], zero_state, pos, emit_state=True)
        states.append(st)
        if l + 1 < depth:
            hm, pos = _experts(hmx, info, layers[l])

    h = x.reshape(batch * seq, D_MODEL)
    pos = None
    for l in range(depth):
        hx, info, _ = _mixer(h, batch, seq, layers[l], states[l], pos)
        h, pos = _experts(hx, info, layers[l])
    return _final_norm(h, pos, norm_final[None]).reshape(batch, seq, D_MODEL)
```

```python
import functools

import jax
import jax.numpy as jnp
import numpy as np
from jax import lax
from jax.experimental import pallas as pl
from jax.experimental.pallas import tpu as pltpu

D_MODEL = 1024
N_META = 16
M_HEADS = 4
M_DQK = 128
M_DV = 256
REF_CHUNK = 64
N_GROUPS = 4
EXP_PER_GROUP = 4
N_PAIRS = 6
N_BUCKETS = N_GROUPS * N_PAIRS
D_EXPERT = 256
EPS = 1e-6
QK_W = M_HEADS * M_DQK
V_W = M_HEADS * M_DV

LANES = 128
SUBLANES = 8
ROW_W = D_MODEL + LANES
ROUTE_ROWS = 32
VMEM_LIMIT = 56 * 1024 * 1024

_OFF_Q = 0
_OFF_K = _OFF_Q + QK_W
_OFF_V = _OFF_K + QK_W
_OFF_GM = _OFF_V + V_W
_OFF_CX = _OFF_GM + V_W
_OFF_CB = _OFF_CX + D_MODEL
_OFF_CC = _OFF_CB + D_MODEL
_OFF_GC = _OFF_CC + D_MODEL
MAIN_W = _OFF_GC + D_MODEL

MIX_CHUNK = 256
MIX_ROWS = 512
EXPERT_ROWS = 256
NORM_ROWS = 256

_F32 = jnp.float32
_BF16 = jnp.bfloat16


def _rms(x, g):
    return x * lax.rsqrt(jnp.mean(x * x, axis=-1, keepdims=True) + EPS) * g


def _dot(a, b):
    return jnp.dot(a, b, preferred_element_type=_F32)


def _split2(x):
    hi = x.astype(_BF16)
    return hi, (x - hi.astype(_F32)).astype(_BF16)


MIN_IDX_TILE = 128


LOOKAHEAD_TILES = 2


def _idx_tiles(idx, rows):
    idx_rows = max(rows, MIN_IDX_TILE)
    tiles = jnp.pad(idx.reshape(-1, rows), ((0, 0), (0, idx_rows - rows)))
    wrap = tiles[jnp.arange(LOOKAHEAD_TILES) % tiles.shape[0]]
    return jnp.concatenate([tiles, wrap], axis=0).reshape(-1), idx_rows


def _row_buffer(rows, width):
    return pltpu.VMEM((2, rows // SUBLANES, SUBLANES, width), _F32)


class RowGather:
    def __init__(self, idx_hbm, src_hbm, idx_smem, buf, sem_i, sem_g, rows):
        self.idx_hbm, self.src_hbm, self.idx_smem, self.buf = idx_hbm, src_hbm, idx_smem, buf
        self.sem_i, self.sem_g, self.rows = sem_i, sem_g, rows
        self.idx_rows = idx_smem.shape[1]

    def _idx_copy(self, s):
        return pltpu.make_async_copy(self.idx_hbm.at[pl.ds(s * self.idx_rows, self.idx_rows)],
                                     self.idx_smem.at[s % 2], self.sem_i.at[s % 2])

    def _issue(self, s):
        def body(g, carry):
            for k in range(SUBLANES):
                pltpu.make_async_copy(self.src_hbm.at[self.idx_smem[s % 2, g * SUBLANES + k]],
                                      self.buf.at[s % 2, g, pl.ds(k, 1)],
                                      self.sem_g.at[s % 2]).start(priority=k % 2)
            return carry
        lax.fori_loop(0, self.rows // SUBLANES, body, 0)

    def _wait_rows(self, s):
        pltpu.make_async_copy(self.buf.at[s % 2], self.buf.at[s % 2], self.sem_g.at[s % 2]).wait()

    def begin(self, step):
        @pl.when(step == 0)
        def _():
            self._idx_copy(step).start()
            self._idx_copy(step).wait()
            self._issue(step)
            self._idx_copy(step + 1).start()

        self._idx_copy(step + 1).wait()
        self._issue(step + 1)
        self._idx_copy(step + 2).start()
        self._wait_rows(step)
        return step % 2

    def drain(self, step):
        self._wait_rows(step)
        self._idx_copy(step + 1).wait()


def _route(lgt):
    g = [lgt[i:i + 1, :] for i in range(N_GROUPS)]
    gmax = jnp.maximum(jnp.maximum(g[0], g[1]), jnp.maximum(g[2], g[3]))
    gsum = sum(jnp.exp(gi - gmax) for gi in g)
    g_val = 1.0 / gsum
    g_idx = jnp.where(g[0] == gmax, 0, jnp.where(g[1] == gmax, 1, jnp.where(g[2] == gmax, 2, 3)))
    e = []
    for k in range(EXP_PER_GROUP):
        col = lambda i, k=k: lgt[N_GROUPS + EXP_PER_GROUP * i + k:N_GROUPS + EXP_PER_GROUP * i + k + 1, :]
        e.append(jnp.where(g_idx == 0, col(0), jnp.where(g_idx == 1, col(1), jnp.where(g_idx == 2, col(2), col(3)))))
    v1 = jnp.maximum(jnp.maximum(e[0], e[1]), jnp.maximum(e[2], e[3]))
    i1 = jnp.where(e[0] == v1, 0, jnp.where(e[1] == v1, 1, jnp.where(e[2] == v1, 2, 3)))
    neg = jnp.float32(-jnp.inf)
    r = [jnp.where(i1 == k, neg, e[k]) for k in range(EXP_PER_GROUP)]
    v2 = jnp.maximum(jnp.maximum(r[0], r[1]), jnp.maximum(r[2], r[3]))
    i2 = jnp.where((r[0] == v2) & (i1 != 0), 0,
                   jnp.where((r[1] == v2) & (i1 != 1), 1, jnp.where((r[2] == v2) & (i1 != 2), 2, 3)))
    ex = jnp.exp(v2 - v1)
    w1 = g_val / (1.0 + ex)
    w2 = g_val * ex / (1.0 + ex)
    lo = jnp.minimum(i1, i2)
    hi = jnp.maximum(i1, i2)
    w_lo = jnp.where(i1 < i2, w1, w2)
    w_hi = jnp.where(i1 < i2, w2, w1)
    pair = jnp.where(lo == 0, hi - 1, jnp.where(lo == 1, hi + 1, 5))
    bucket = (g_idx * N_PAIRS + pair).astype(_F32)
    row = lax.broadcasted_iota(jnp.int32, (SUBLANES, lgt.shape[1]), 0)
    return jnp.where(row == 0, bucket, jnp.where(row == 1, w_lo, jnp.where(row == 2, w_hi, 0.0)))


def _mixer_kernel(*refs, rows, chunk, gathered, emit_state):
    refs = list(refs)
    take = lambda n: [refs.pop(0) for _ in range(n)]
    if gathered:
        idx_hbm, src_hbm = take(2)
    else:
        h_ref, = take(1)
    (nmix_ref, wmain_ref, wif_ref, bif_ref, convw_ref, gain_ref, wout_ref, nffn_ref, wrt_ref,
     brt_ref, c0_ref, n0_ref, m0_ref, u0_ref) = take(14)
    out_ref, info_ref = take(2)
    if emit_state:
        cN_ref, nN_ref, mN_ref, uN_ref = take(4)
    c_scr, n_scr, m_scr, ubuf, ybuf = take(5)
    t = pl.program_id(1)

    @pl.when(t == 0)
    def _():
        c_scr[...] = c0_ref[...]
        n_scr[...] = n0_ref[...]
        m_scr[...] = m0_ref[...]
        ubuf[0:SUBLANES, :] = u0_ref[...]

    if gathered:
        idx_smem, hbuf, sem_i, sem_g = take(4)
        gather = RowGather(idx_hbm, src_hbm, idx_smem, hbuf, sem_i, sem_g, rows)
        step = pl.program_id(0) * pl.num_programs(1) + t
        slot = gather.begin(step)
        h = hbuf[slot].reshape(rows, D_MODEL)
    else:
        h = h_ref[...]
    xn = _rms(h, nmix_ref[...]).astype(_BF16)

    def proj(off, width):
        return _dot(xn, wmain_ref[:, off:off + width])

    u = proj(_OFF_CC, D_MODEL) * proj(_OFF_CX, D_MODEL)
    ubuf[SUBLANES:SUBLANES + rows, :] = u
    conv = (convw_ref[0:1, :] * ubuf[SUBLANES - 2:SUBLANES - 2 + rows, :]
            + convw_ref[1:2, :] * ubuf[SUBLANES - 1:SUBLANES - 1 + rows, :]
            + convw_ref[2:3, :] * u)
    ubuf[0:SUBLANES, :] = ubuf[rows:rows + SUBLANES, :]
    ybuf[...] = jax.nn.sigmoid(proj(_OFF_GC, D_MODEL)) * (proj(_OFF_CB, D_MODEL) * conv)

    gates = _dot(xn, wif_ref[...]) + bif_ref[...]
    lane = lax.broadcasted_iota(jnp.int32, gates.shape, 1)
    is_f = (lane >= M_HEADS) & (lane < 2 * M_HEADS)
    logf = jnp.where(is_f, jax.nn.log_sigmoid(gates), 0.0)
    ri = lax.broadcasted_iota(jnp.int32, (rows, rows), 0)
    ci = lax.broadcasted_iota(jnp.int32, (rows, rows), 1)
    shift = chunk.bit_length() - 1
    tri = (((ri >> shift) == (ci >> shift)) & (ci <= ri)).astype(_BF16)
    f_hi = logf.astype(_BF16)
    f_r = logf - f_hi.astype(_F32)
    f_mid = f_r.astype(_BF16)
    f_lo = (f_r - f_mid.astype(_F32)).astype(_BF16)
    bcum = _dot(tri, f_hi) + _dot(tri, f_mid) + _dot(tri, f_lo)
    gates_t = gates.T
    bcum_t = bcum.T

    q_all = (proj(_OFF_Q, QK_W) * (M_DQK ** -0.5)).astype(_BF16)
    k_all = proj(_OFF_K, QK_W)
    v_all = proj(_OFF_V, V_W).astype(_BF16)
    gm_all = proj(_OFF_GM, V_W)

    ti = lax.broadcasted_iota(jnp.int32, (chunk, chunk), 0)
    si = lax.broadcasted_iota(jnp.int32, (chunk, chunk), 1)
    causal = si <= ti

    for c in range(rows // chunk):
        r0 = c * chunk
        for hd in range(M_HEADS):
            q = q_all[r0:r0 + chunk, hd * M_DQK:(hd + 1) * M_DQK]
            kf = k_all[r0:r0 + chunk, hd * M_DQK:(hd + 1) * M_DQK]
            v = v_all[r0:r0 + chunk, hd * M_DV:(hd + 1) * M_DV]
            b_c = bcum[r0:r0 + chunk, M_HEADS + hd:M_HEADS + hd + 1]
            i_c = gates[r0:r0 + chunk, hd:hd + 1]
            b_r = bcum_t[M_HEADS + hd:M_HEADS + hd + 1, r0:r0 + chunk]
            i_r = gates_t[hd:hd + 1, r0:r0 + chunk]
            m_prev = m_scr[hd:hd + 1, 0:1]
            c_prev = c_scr[hd]
            n_prev = n_scr[hd:hd + 1, :]

            log_d = jnp.where(causal, b_c - b_r + i_r, -jnp.inf)
            inter = b_c + m_prev
            m_t = jnp.maximum(inter, jnp.max(log_d, axis=-1, keepdims=True))
            dmat = jnp.exp(log_d - m_t)
            s = lax.dot_general(q, kf.astype(_BF16), (((1,), (1,)), ((), ())),
                                preferred_element_type=_F32) * dmat
            e_inter = jnp.exp(inter - m_t)
            num = _dot(s.astype(_BF16), v) + e_inter * _dot(q, c_prev.astype(_BF16))
            den = (jnp.sum(s, axis=-1, keepdims=True)
                   + e_inter * jnp.sum(q.astype(_F32) * n_prev, axis=-1, keepdims=True))
            hc = num / jnp.maximum(jnp.abs(den), jnp.exp(-m_t))

            b_last = b_c[chunk - 1:chunk, :]
            log_w = b_last - b_c + i_c
            m_new = jnp.maximum(b_last + m_prev, jnp.max(log_w, axis=0, keepdims=True))
            kw = kf * jnp.exp(log_w - m_new)
            decay = jnp.exp(b_last + m_prev - m_new)
            c_scr[hd] = decay * c_prev + lax.dot_general(
                kw.astype(_BF16), v, (((0,), (0,)), ((), ())), preferred_element_type=_F32)
            n_scr[hd:hd + 1, :] = decay * n_prev + jnp.sum(kw, axis=0, keepdims=True)
            m_scr[hd:hd + 1, :] = jnp.broadcast_to(m_new, (1, LANES))

            hn = hc * lax.rsqrt(jnp.mean(hc * hc, axis=-1, keepdims=True) + EPS)
            cols = slice(hd * M_DV, (hd + 1) * M_DV)
            ybuf[r0:r0 + chunk, cols] += (jax.nn.sigmoid(gm_all[r0:r0 + chunk, cols])
                                          * (hn * gain_ref[:, cols]))

    h1 = h + _dot(ybuf[...].astype(_BF16), wout_ref[...])

    x_hi, x_lo = _split2(_rms(h1, nffn_ref[...]))
    nt = (((1,), (1,)), ((), ()))
    hi_all = lax.dot_general(wrt_ref[...], x_hi, nt, preferred_element_type=_F32)
    lo_x = lax.dot_general(wrt_ref[0:ROUTE_ROWS, :], x_lo, nt, preferred_element_type=_F32)
    lgt = hi_all[0:ROUTE_ROWS] + hi_all[ROUTE_ROWS:2 * ROUTE_ROWS] + lo_x + brt_ref[...]
    info_t = _route(lgt)
    info_ref[0] = info_t
    out_ref[:, 0, 0:D_MODEL] = h1
    out_ref[:, 0, D_MODEL:ROW_W] = jnp.concatenate(
        [info_t, jnp.zeros((LANES - SUBLANES, rows), _F32)], axis=0).T

    if gathered:
        @pl.when(step == pl.num_programs(0) * pl.num_programs(1) - 1)
        def _():
            gather.drain(step + 1)

    if emit_state:
        @pl.when(t == pl.num_programs(1) - 1)
        def _():
            cN_ref[...] = c_scr[...]
            nN_ref[...] = n_scr[...]
            mN_ref[...] = m_scr[...]
            uN_ref[...] = ubuf[0:SUBLANES, :]


def _mixer(h, seqs, seq_len, lw, state, pos=None, emit_state=False):
    rows = min(MIX_ROWS, seq_len)
    chunk = min(MIX_CHUNK, rows)
    steps = seq_len // rows
    gathered = pos is not None
    const = lambda shape: pl.BlockSpec(shape, lambda b, t: (0,) * len(shape), pipeline_mode=pl.Buffered(1))
    any_spec = pl.BlockSpec(memory_space=pl.ANY)
    state_shapes = [(M_HEADS, M_DQK, M_DV), (SUBLANES, M_DQK), (SUBLANES, LANES), (SUBLANES, D_MODEL)]
    if gathered:
        idx, idx_rows = _idx_tiles(pos, rows)
        in_specs, args = [any_spec, any_spec], [idx, h]
    else:
        in_specs, args = [pl.BlockSpec((rows, D_MODEL), lambda b, t: (b * steps + t, 0))], [h]
    in_specs += [
        const((1, D_MODEL)), const((D_MODEL, MAIN_W)), const((D_MODEL, LANES)), const((1, LANES)),
        const((3, D_MODEL)), const((1, V_W)), const((D_MODEL, D_MODEL)), const((1, D_MODEL)),
        const((2 * ROUTE_ROWS, D_MODEL)), const((ROUTE_ROWS, 1)),
    ] + [const(s) for s in state_shapes]
    out_specs = [pl.BlockSpec((rows, 1, ROW_W), lambda b, t: (b * steps + t, 0, 0)),
                 pl.BlockSpec((1, SUBLANES, rows), lambda b, t: (b * steps + t, 0, 0))]
    out_shape = [jax.ShapeDtypeStruct((seqs * seq_len, 1, ROW_W), _F32),
                 jax.ShapeDtypeStruct((seqs * steps, SUBLANES, rows), _F32)]
    if emit_state:
        out_specs += [pl.BlockSpec(s, lambda b, t, n=len(s): (0,) * n) for s in state_shapes]
        out_shape += [jax.ShapeDtypeStruct(s, _F32) for s in state_shapes]
    scratch = [
        pltpu.VMEM((M_HEADS, M_DQK, M_DV), _F32),
        pltpu.VMEM((SUBLANES, M_DQK), _F32),
        pltpu.VMEM((SUBLANES, LANES), _F32),
        pltpu.VMEM((rows + SUBLANES, D_MODEL), _F32),
        pltpu.VMEM((rows, D_MODEL), _F32),
    ]
    if gathered:
        scratch += [pltpu.SMEM((2, idx_rows), jnp.int32), _row_buffer(rows, D_MODEL),
                    pltpu.SemaphoreType.DMA((2,)), pltpu.SemaphoreType.DMA((2,))]
    outs = pl.pallas_call(
        functools.partial(_mixer_kernel, rows=rows, chunk=chunk, gathered=gathered, emit_state=emit_state),
        grid=(seqs, steps),
        in_specs=in_specs,
        out_specs=out_specs,
        out_shape=out_shape,
        scratch_shapes=scratch,
        compiler_params=pltpu.CompilerParams(
            dimension_semantics=("arbitrary", "arbitrary"), vmem_limit_bytes=VMEM_LIMIT),
        name="mixer",
    )(*args, lw["norm_mix"], lw["w_main"], lw["w_if"], lw["b_if"], lw["conv_w"], lw["mh_gain"], lw["w_out"],
      lw["norm_ffn"], lw["w_route_t"], lw["b_route_t"], *state)
    return outs[0], outs[1], tuple(outs[2:])


def _expert_kernel(elo_ref, ehi_ref, nused_ref, inv_hbm, hx_hbm, nffn_ref,
                   wg_lo, wg_hi, wu_lo, wu_hi, wd_lo, wd_hi, out_ref,
                   idx_smem, xbuf, sem_i, sem_g, *, tile):
    j = pl.program_id(0)
    n_used = nused_ref[0]
    gather = RowGather(inv_hbm, hx_hbm, idx_smem, xbuf, sem_i, sem_g, tile)

    @pl.when(j < n_used)
    def _():
        slot = gather.begin(j)
        x = xbuf[slot].reshape(tile, ROW_W)
        hrow = x[:, 0:D_MODEL]
        w_lo = x[:, D_MODEL + 1:D_MODEL + 2]
        w_hi = x[:, D_MODEL + 2:D_MODEL + 3]
        xn = _rms(hrow, nffn_ref[...]).astype(_BF16)
        hid_lo = jax.nn.silu(_dot(xn, wg_lo[0])) * _dot(xn, wu_lo[0]) * w_lo
        hid_hi = jax.nn.silu(_dot(xn, wg_hi[0])) * _dot(xn, wu_hi[0]) * w_hi
        out_ref[:, 0, :] = hrow + _dot(hid_lo.astype(_BF16), wd_lo[0]) + _dot(hid_hi.astype(_BF16), wd_hi[0])

    @pl.when(j == n_used)
    def _():
        gather.drain(j)

    @pl.when(j >= n_used)
    def _():
        out_ref[...] = jnp.zeros_like(out_ref)


_PAIR_LO = np.array([0, 0, 0, 1, 1, 2], np.int32)
_PAIR_HI = np.array([1, 2, 3, 2, 3, 3], np.int32)


def _sort_plan(bucket, n_tokens, tile):
    n_tiles = -(-n_tokens // tile) + N_BUCKETS + 1
    ids = jnp.arange(N_BUCKETS, dtype=jnp.int32)
    order = jnp.argsort(bucket, stable=True).astype(jnp.int32)
    rank = jnp.argsort(order).astype(jnp.int32)
    counts = jnp.sum((bucket[:, None] == ids[None, :]).astype(jnp.int32), axis=0)
    tiles = (counts + tile - 1) // tile
    tile_end = jnp.cumsum(tiles)
    tile_start = tile_end - tiles
    cnt_start = jnp.cumsum(counts) - counts
    n_used = tile_end[-1]
    shift = tile_start * tile - cnt_start
    pos = rank + jnp.sum(jnp.where(bucket[:, None] == ids[None, :], shift[None, :], 0), axis=1)
    j = jnp.arange(n_tiles, dtype=jnp.int32)
    last = jnp.maximum(n_used - 1, 0)
    jj = jnp.minimum(j, last)
    tb = jnp.sum((jj[:, None] >= tile_end[None, :]).astype(jnp.int32), axis=1)
    r = jnp.arange(n_tiles * tile, dtype=jnp.int32)
    rb = jnp.repeat(tb, tile)
    rsel = rb[:, None] == ids[None, :]
    slot = r - jnp.sum(jnp.where(rsel, shift[None, :], 0), axis=1)
    slot_end = jnp.sum(jnp.where(rsel, (cnt_start + counts)[None, :], 0), axis=1)
    valid = (slot < slot_end) & (jnp.repeat(j, tile) < n_used)
    inv = jnp.where(valid, order[jnp.clip(slot, 0, n_tokens - 1)], r % n_tokens).astype(jnp.int32)
    grp = tb // N_PAIRS
    pr = tb % N_PAIRS
    e_lo = (grp * EXP_PER_GROUP + jnp.asarray(_PAIR_LO)[pr]).astype(jnp.int32)
    e_hi = (grp * EXP_PER_GROUP + jnp.asarray(_PAIR_HI)[pr]).astype(jnp.int32)
    return e_lo, e_hi, n_used.reshape(1).astype(jnp.int32), inv, pos.astype(jnp.int32), n_tiles


def _experts(hx, info, lw):
    n_tokens = hx.shape[0]
    tile = min(EXPERT_ROWS, n_tokens)
    bucket = info[:, 0, :].reshape(n_tokens).astype(jnp.int32)
    e_lo, e_hi, n_used, inv, pos, n_tiles = _sort_plan(bucket, n_tokens, tile)
    inv, idx_rows = _idx_tiles(inv, tile)
    any_spec = pl.BlockSpec(memory_space=pl.ANY)
    vec = pl.BlockSpec((1, D_MODEL), lambda j, lo, hi, nu: (0, 0))
    w_in_lo = pl.BlockSpec((1, D_MODEL, D_EXPERT), lambda j, lo, hi, nu: (lo[j], 0, 0))
    w_in_hi = pl.BlockSpec((1, D_MODEL, D_EXPERT), lambda j, lo, hi, nu: (hi[j], 0, 0))
    w_dn_lo = pl.BlockSpec((1, D_EXPERT, D_MODEL), lambda j, lo, hi, nu: (lo[j], 0, 0))
    w_dn_hi = pl.BlockSpec((1, D_EXPERT, D_MODEL), lambda j, lo, hi, nu: (hi[j], 0, 0))
    ys = pl.pallas_call(
        functools.partial(_expert_kernel, tile=tile),
        grid_spec=pltpu.PrefetchScalarGridSpec(
            num_scalar_prefetch=3,
            grid=(n_tiles,),
            in_specs=[any_spec, any_spec, vec, w_in_lo, w_in_hi, w_in_lo, w_in_hi, w_dn_lo, w_dn_hi],
            out_specs=pl.BlockSpec((tile, 1, D_MODEL), lambda j, lo, hi, nu: (j, 0, 0)),
            scratch_shapes=[
                pltpu.SMEM((2, idx_rows), jnp.int32),
                _row_buffer(tile, ROW_W),
                pltpu.SemaphoreType.DMA((2,)),
                pltpu.SemaphoreType.DMA((2,)),
            ],
        ),
        out_shape=jax.ShapeDtypeStruct((n_tiles * tile, 1, D_MODEL), _F32),
        compiler_params=pltpu.CompilerParams(
            dimension_semantics=("arbitrary",), vmem_limit_bytes=VMEM_LIMIT),
        name="experts",
    )(e_lo, e_hi, n_used, inv, hx, lw["norm_ffn"],
      lw["w_gate"], lw["w_gate"], lw["w_up"], lw["w_up"], lw["w_down"], lw["w_down"])
    return ys, pos


def _final_norm_kernel(pos_hbm, ys_hbm, g_ref, out_ref, idx_smem, buf, sem_i, sem_g, *, rows):
    step = pl.program_id(0)
    gather = RowGather(pos_hbm, ys_hbm, idx_smem, buf, sem_i, sem_g, rows)
    slot = gather.begin(step)
    out_ref[...] = _rms(buf[slot].reshape(rows, D_MODEL), g_ref[...])

    @pl.when(step == pl.num_programs(0) - 1)
    def _():
        gather.drain(step + 1)


def _final_norm(ys, pos, g):
    n_tokens = pos.shape[0]
    rows = NORM_ROWS
    pos, idx_rows = _idx_tiles(pos, rows)
    any_spec = pl.BlockSpec(memory_space=pl.ANY)
    return pl.pallas_call(
        functools.partial(_final_norm_kernel, rows=rows),
        grid=(n_tokens // rows,),
        in_specs=[any_spec, any_spec, pl.BlockSpec((1, D_MODEL), lambda i: (0, 0))],
        out_specs=pl.BlockSpec((rows, D_MODEL), lambda i: (i, 0)),
        out_shape=jax.ShapeDtypeStruct((n_tokens, D_MODEL), _F32),
        scratch_shapes=[pltpu.SMEM((2, idx_rows), jnp.int32), _row_buffer(rows, D_MODEL),
                        pltpu.SemaphoreType.DMA((2,)), pltpu.SemaphoreType.DMA((2,))],
        compiler_params=pltpu.CompilerParams(dimension_semantics=("arbitrary",)),
        name="final_norm",
    )(pos, ys, g)


def _layer_weights(l, norm_mix, w_in, b_if, conv_w, mh_gain, w_out, norm_ffn,
                   w_group, b_group, w_router, b_router, w_gate, w_up, w_down):
    w = w_in[l]
    o_if = 2 * QK_W + V_W
    o_rest = o_if + 2 * M_HEADS
    w_main = jnp.concatenate([w[:, :o_if], w[:, o_rest:]], axis=1).astype(_BF16)
    w_if = jnp.pad(w[:, o_if:o_rest], ((0, 0), (0, LANES - 2 * M_HEADS))).astype(_BF16)
    pad_lanes = lambda a: jnp.pad(a, ((0, 0), (0, LANES - a.shape[1])))
    n_logits = N_GROUPS + N_GROUPS * EXP_PER_GROUP
    w_route = jnp.pad(jnp.concatenate([w_group[l], w_router[l]], axis=1).T, ((0, ROUTE_ROWS - n_logits), (0, 0)))
    w_route_hi = w_route.astype(_BF16)
    w_route_lo = (w_route - w_route_hi.astype(_F32)).astype(_BF16)
    b_route = jnp.pad(jnp.concatenate([b_group[l], b_router[l]]), (0, ROUTE_ROWS - n_logits))
    return {
        "norm_mix": norm_mix[l][None], "w_main": w_main, "w_if": w_if,
        "b_if": pad_lanes(b_if[l][None]), "conv_w": conv_w[l], "mh_gain": mh_gain[l][None],
        "w_out": w_out[l].astype(_BF16), "norm_ffn": norm_ffn[l][None],
        "w_route_t": jnp.concatenate([w_route_hi, w_route_lo], axis=0), "b_route_t": b_route[:, None],
        "w_gate": w_gate[l].astype(_BF16), "w_up": w_up[l].astype(_BF16), "w_down": w_down[l].astype(_BF16),
    }


def kernel(x, meta_tokens, norm_mix, w_in, b_if, conv_w, mh_gain, w_out, norm_ffn, w_group, b_group,
           w_router, b_router, w_gate, w_up, w_down, norm_final):
    batch, seq, d = x.shape
    depth = w_in.shape[0]
    assert d == D_MODEL and seq % MIX_ROWS == 0 and meta_tokens.shape == (N_META, D_MODEL)
    assert (batch * seq) % NORM_ROWS == 0
    layers = [_layer_weights(l, norm_mix, w_in, b_if, conv_w, mh_gain, w_out, norm_ffn,
                             w_group, b_group, w_router, b_router, w_gate, w_up, w_down)
              for l in range(depth)]
    zero_state = (jnp.zeros((M_HEADS, M_DQK, M_DV), _F32), jnp.zeros((SUBLANES, M_DQK), _F32),
                  jnp.zeros((SUBLANES, LANES), _F32), jnp.zeros((SUBLANES, D_MODEL), _F32))

    hm = jnp.concatenate([jnp.zeros((REF_CHUNK - N_META, D_MODEL), _F32), meta_tokens.astype(_F32)], axis=0)
    pos = None
    states = []
    for l in range(depth):
        hmx, info, st = _mixer(hm, 1, REF_CHUNK, layers[l], zero_state, pos, emit_state=True)
        states.append(st)
        if l + 1 < depth:
            hm, pos = _experts(hmx, info, layers[l])

    h = x.reshape(batch * seq, D_MODEL)
    pos = None
    for l in range(depth):
        hx, info, _ = _mixer(h, batch, seq, layers[l], states[l], pos)
        h, pos = _experts(hx, info, layers[l])
    return _final_norm(h, pos, norm_final[None]).reshape(batch, seq, D_MODEL)
```

```python
import functools

import jax
import jax.numpy as jnp
import numpy as np
from jax import lax
from jax.experimental import pallas as pl
from jax.experimental.pallas import tpu as pltpu

D_MODEL = 1024
N_META = 16
M_HEADS = 4
M_DQK = 128
M_DV = 256
REF_CHUNK = 64
N_GROUPS = 4
EXP_PER_GROUP = 4
N_PAIRS = 6
N_BUCKETS = N_GROUPS * N_PAIRS
D_EXPERT = 256
EPS = 1e-6
QK_W = M_HEADS * M_DQK
V_W = M_HEADS * M_DV

LANES = 128
SUBLANES = 8
ROW_W = D_MODEL + LANES
ROUTE_ROWS = 32
VMEM_LIMIT = 56 * 1024 * 1024

_OFF_Q = 0
_OFF_K = _OFF_Q + QK_W
_OFF_V = _OFF_K + QK_W
_OFF_GM = _OFF_V + V_W
_OFF_CX = _OFF_GM + V_W
_OFF_CB = _OFF_CX + D_MODEL
_OFF_CC = _OFF_CB + D_MODEL
_OFF_GC = _OFF_CC + D_MODEL
MAIN_W = _OFF_GC + D_MODEL

MIX_CHUNK = 256
MIX_ROWS = 512
EXPERT_ROWS = 256

_F32 = jnp.float32
_BF16 = jnp.bfloat16


def _rms(x, g):
    return x * lax.rsqrt(jnp.mean(x * x, axis=-1, keepdims=True) + EPS) * g


def _dot(a, b):
    return jnp.dot(a, b, preferred_element_type=_F32)


def _split2(x):
    hi = x.astype(_BF16)
    return hi, (x - hi.astype(_F32)).astype(_BF16)


MIN_IDX_TILE = 128


LOOKAHEAD_TILES = 2
IDX_SLOTS = 3


def _idx_tiles(idx, rows):
    idx_rows = max(rows, MIN_IDX_TILE)
    tiles = jnp.pad(idx.reshape(-1, rows), ((0, 0), (0, idx_rows - rows)))
    wrap = tiles[jnp.arange(LOOKAHEAD_TILES) % tiles.shape[0]]
    return jnp.concatenate([tiles, wrap], axis=0).reshape(-1), idx_rows


def _row_buffer(rows, width):
    return pltpu.VMEM((2, rows // SUBLANES, SUBLANES, width), _F32)


PHASES = 2 * IDX_SLOTS


def _by_phase(step, fn):
    for phase in range(PHASES):
        pl.when(step % PHASES == phase)(functools.partial(fn, phase))


class RowGather:
    def __init__(self, idx_hbm, src_hbm, idx_smem, buf, sem_i, sem_g, rows):
        self.idx_hbm, self.src_hbm, self.idx_smem, self.buf = idx_hbm, src_hbm, idx_smem, buf
        self.sem_i, self.sem_g, self.rows = sem_i, sem_g, rows
        self.idx_rows = idx_smem.shape[1]

    def _idx_copy(self, s):
        return pltpu.make_async_copy(self.idx_hbm.at[pl.ds(s * self.idx_rows, self.idx_rows)],
                                     self.idx_smem.at[s % IDX_SLOTS], self.sem_i.at[s % IDX_SLOTS])

    def _issue(self, s):
        def issue(phase):
            i_slot, b_slot = phase % IDX_SLOTS, phase % 2

            def body(g, carry):
                for k in range(SUBLANES):
                    pltpu.make_async_copy(self.src_hbm.at[self.idx_smem[i_slot, g * SUBLANES + k]],
                                          self.buf.at[b_slot, g, pl.ds(k, 1)], self.sem_g.at[b_slot]).start()
                return carry
            lax.fori_loop(0, self.rows // SUBLANES, body, 0)
        _by_phase(s, issue)

    def _wait_rows(self, s):
        pltpu.make_async_copy(self.buf.at[s % 2], self.buf.at[s % 2], self.sem_g.at[s % 2]).wait()

    def begin(self, step):
        @pl.when(step == 0)
        def _():
            self._idx_copy(step).start()
            self._idx_copy(step).wait()
            self._issue(step)
            self._idx_copy(step + 1).start()

        self._idx_copy(step + 1).wait()
        self._issue(step + 1)
        self._idx_copy(step + 2).start()
        self._wait_rows(step)
        return step % 2

    def drain(self, step):
        self._wait_rows(step)
        self._idx_copy(step + 1).wait()


def _route(lgt):
    g = [lgt[i:i + 1, :] for i in range(N_GROUPS)]
    gmax = jnp.maximum(jnp.maximum(g[0], g[1]), jnp.maximum(g[2], g[3]))
    gsum = sum(jnp.exp(gi - gmax) for gi in g)
    g_val = 1.0 / gsum
    g_idx = jnp.where(g[0] == gmax, 0, jnp.where(g[1] == gmax, 1, jnp.where(g[2] == gmax, 2, 3)))
    e = []
    for k in range(EXP_PER_GROUP):
        col = lambda i, k=k: lgt[N_GROUPS + EXP_PER_GROUP * i + k:N_GROUPS + EXP_PER_GROUP * i + k + 1, :]
        e.append(jnp.where(g_idx == 0, col(0), jnp.where(g_idx == 1, col(1), jnp.where(g_idx == 2, col(2), col(3)))))
    v1 = jnp.maximum(jnp.maximum(e[0], e[1]), jnp.maximum(e[2], e[3]))
    i1 = jnp.where(e[0] == v1, 0, jnp.where(e[1] == v1, 1, jnp.where(e[2] == v1, 2, 3)))
    neg = jnp.float32(-jnp.inf)
    r = [jnp.where(i1 == k, neg, e[k]) for k in range(EXP_PER_GROUP)]
    v2 = jnp.maximum(jnp.maximum(r[0], r[1]), jnp.maximum(r[2], r[3]))
    i2 = jnp.where((r[0] == v2) & (i1 != 0), 0,
                   jnp.where((r[1] == v2) & (i1 != 1), 1, jnp.where((r[2] == v2) & (i1 != 2), 2, 3)))
    ex = jnp.exp(v2 - v1)
    w1 = g_val / (1.0 + ex)
    w2 = g_val * ex / (1.0 + ex)
    lo = jnp.minimum(i1, i2)
    hi = jnp.maximum(i1, i2)
    w_lo = jnp.where(i1 < i2, w1, w2)
    w_hi = jnp.where(i1 < i2, w2, w1)
    pair = jnp.where(lo == 0, hi - 1, jnp.where(lo == 1, hi + 1, 5))
    bucket = (g_idx * N_PAIRS + pair).astype(_F32)
    row = lax.broadcasted_iota(jnp.int32, (SUBLANES, lgt.shape[1]), 0)
    return jnp.where(row == 0, bucket, jnp.where(row == 1, w_lo, jnp.where(row == 2, w_hi, 0.0)))


def _mixer_kernel(*refs, rows, chunk, emit_state):
    refs = list(refs)
    take = lambda n: [refs.pop(0) for _ in range(n)]
    (h_ref, nmix_ref, wmain_ref, wif_ref, bif_ref, convw_ref, gain_ref, wout_ref, nffn_ref, wrt_ref,
     brt_ref, c0_ref, n0_ref, m0_ref, u0_ref) = take(15)
    out_ref, info_ref = take(2)
    if emit_state:
        cN_ref, nN_ref, mN_ref, uN_ref = take(4)
    c_scr, n_scr, m_scr, ubuf, ybuf = take(5)
    t = pl.program_id(1)

    @pl.when(t == 0)
    def _():
        c_scr[...] = c0_ref[...]
        n_scr[...] = n0_ref[...]
        m_scr[...] = m0_ref[...]
        ubuf[0:SUBLANES, :] = u0_ref[...]

    h = h_ref[...]
    xn = _rms(h, nmix_ref[...]).astype(_BF16)

    def proj(off, width):
        return _dot(xn, wmain_ref[:, off:off + width])

    u = proj(_OFF_CC, D_MODEL) * proj(_OFF_CX, D_MODEL)
    ubuf[SUBLANES:SUBLANES + rows, :] = u
    conv = (convw_ref[0:1, :] * ubuf[SUBLANES - 2:SUBLANES - 2 + rows, :]
            + convw_ref[1:2, :] * ubuf[SUBLANES - 1:SUBLANES - 1 + rows, :]
            + convw_ref[2:3, :] * u)
    ubuf[0:SUBLANES, :] = ubuf[rows:rows + SUBLANES, :]
    ybuf[...] = jax.nn.sigmoid(proj(_OFF_GC, D_MODEL)) * (proj(_OFF_CB, D_MODEL) * conv)

    gates = _dot(xn, wif_ref[...]) + bif_ref[...]
    lane = lax.broadcasted_iota(jnp.int32, gates.shape, 1)
    is_f = (lane >= M_HEADS) & (lane < 2 * M_HEADS)
    logf = jnp.where(is_f, jax.nn.log_sigmoid(gates), 0.0)
    ri = lax.broadcasted_iota(jnp.int32, (rows, rows), 0)
    ci = lax.broadcasted_iota(jnp.int32, (rows, rows), 1)
    shift = chunk.bit_length() - 1
    tri = (((ri >> shift) == (ci >> shift)) & (ci <= ri)).astype(_BF16)
    f_hi = logf.astype(_BF16)
    f_r = logf - f_hi.astype(_F32)
    f_mid = f_r.astype(_BF16)
    f_lo = (f_r - f_mid.astype(_F32)).astype(_BF16)
    bcum = _dot(tri, f_hi) + _dot(tri, f_mid) + _dot(tri, f_lo)
    gates_t = gates.T
    bcum_t = bcum.T

    q_all = (proj(_OFF_Q, QK_W) * (M_DQK ** -0.5)).astype(_BF16)
    k_all = proj(_OFF_K, QK_W)
    v_all = proj(_OFF_V, V_W).astype(_BF16)
    gm_all = proj(_OFF_GM, V_W)

    ti = lax.broadcasted_iota(jnp.int32, (chunk, chunk), 0)
    si = lax.broadcasted_iota(jnp.int32, (chunk, chunk), 1)
    causal = si <= ti

    for c in range(rows // chunk):
        r0 = c * chunk
        for hd in range(M_HEADS):
            q = q_all[r0:r0 + chunk, hd * M_DQK:(hd + 1) * M_DQK]
            kf = k_all[r0:r0 + chunk, hd * M_DQK:(hd + 1) * M_DQK]
            v = v_all[r0:r0 + chunk, hd * M_DV:(hd + 1) * M_DV]
            b_c = bcum[r0:r0 + chunk, M_HEADS + hd:M_HEADS + hd + 1]
            i_c = gates[r0:r0 + chunk, hd:hd + 1]
            b_r = bcum_t[M_HEADS + hd:M_HEADS + hd + 1, r0:r0 + chunk]
            i_r = gates_t[hd:hd + 1, r0:r0 + chunk]
            m_prev = m_scr[hd:hd + 1, 0:1]
            c_prev = c_scr[hd]
            n_prev = n_scr[hd:hd + 1, :]

            log_d = jnp.where(causal, b_c - b_r + i_r, -jnp.inf)
            inter = b_c + m_prev
            m_t = jnp.maximum(inter, jnp.max(log_d, axis=-1, keepdims=True))
            dmat = jnp.exp(log_d - m_t)
            s = lax.dot_general(q, kf.astype(_BF16), (((1,), (1,)), ((), ())),
                                preferred_element_type=_F32) * dmat
            e_inter = jnp.exp(inter - m_t)
            num = _dot(s.astype(_BF16), v) + e_inter * _dot(q, c_prev.astype(_BF16))
            den = (jnp.sum(s, axis=-1, keepdims=True)
                   + e_inter * jnp.sum(q.astype(_F32) * n_prev, axis=-1, keepdims=True))
            hc = num / jnp.maximum(jnp.abs(den), jnp.exp(-m_t))

            b_last = b_c[chunk - 1:chunk, :]
            log_w = b_last - b_c + i_c
            m_new = jnp.maximum(b_last + m_prev, jnp.max(log_w, axis=0, keepdims=True))
            kw = kf * jnp.exp(log_w - m_new)
            decay = jnp.exp(b_last + m_prev - m_new)
            c_scr[hd] = decay * c_prev + lax.dot_general(
                kw.astype(_BF16), v, (((0,), (0,)), ((), ())), preferred_element_type=_F32)
            n_scr[hd:hd + 1, :] = decay * n_prev + jnp.sum(kw, axis=0, keepdims=True)
            m_scr[hd:hd + 1, :] = jnp.broadcast_to(m_new, (1, LANES))

            hn = hc * lax.rsqrt(jnp.mean(hc * hc, axis=-1, keepdims=True) + EPS)
            cols = slice(hd * M_DV, (hd + 1) * M_DV)
            ybuf[r0:r0 + chunk, cols] += (jax.nn.sigmoid(gm_all[r0:r0 + chunk, cols])
                                          * (hn * gain_ref[:, cols]))

    h1 = h + _dot(ybuf[...].astype(_BF16), wout_ref[...])

    x_hi, x_lo = _split2(_rms(h1, nffn_ref[...]))
    nt = (((1,), (1,)), ((), ()))
    hi_all = lax.dot_general(wrt_ref[...], x_hi, nt, preferred_element_type=_F32)
    lo_x = lax.dot_general(wrt_ref[0:ROUTE_ROWS, :], x_lo, nt, preferred_element_type=_F32)
    lgt = hi_all[0:ROUTE_ROWS] + hi_all[ROUTE_ROWS:2 * ROUTE_ROWS] + lo_x + brt_ref[...]
    info_t = _route(lgt)
    info_ref[0] = info_t
    out_ref[:, 0, 0:D_MODEL] = h1
    out_ref[:, 0, D_MODEL:ROW_W] = jnp.concatenate(
        [info_t, jnp.zeros((LANES - SUBLANES, rows), _F32)], axis=0).T

    if emit_state:
        @pl.when(t == pl.num_programs(1) - 1)
        def _():
            cN_ref[...] = c_scr[...]
            nN_ref[...] = n_scr[...]
            mN_ref[...] = m_scr[...]
            uN_ref[...] = ubuf[0:SUBLANES, :]


def _mixer(h, seqs, seq_len, lw, state, emit_state=False):
    rows = min(MIX_ROWS, seq_len)
    chunk = min(MIX_CHUNK, rows)
    steps = seq_len // rows
    const = lambda shape: pl.BlockSpec(shape, lambda b, t: (0,) * len(shape), pipeline_mode=pl.Buffered(1))
    state_shapes = [(M_HEADS, M_DQK, M_DV), (SUBLANES, M_DQK), (SUBLANES, LANES), (SUBLANES, D_MODEL)]
    in_specs = [
        pl.BlockSpec((rows, D_MODEL), lambda b, t: (b * steps + t, 0)),
        const((1, D_MODEL)), const((D_MODEL, MAIN_W)), const((D_MODEL, LANES)), const((1, LANES)),
        const((3, D_MODEL)), const((1, V_W)), const((D_MODEL, D_MODEL)), const((1, D_MODEL)),
        const((2 * ROUTE_ROWS, D_MODEL)), const((ROUTE_ROWS, 1)),
    ] + [const(s) for s in state_shapes]
    out_specs = [pl.BlockSpec((rows, 1, ROW_W), lambda b, t: (b * steps + t, 0, 0)),
                 pl.BlockSpec((1, SUBLANES, rows), lambda b, t: (b * steps + t, 0, 0))]
    out_shape = [jax.ShapeDtypeStruct((seqs * seq_len, 1, ROW_W), _F32),
                 jax.ShapeDtypeStruct((seqs * steps, SUBLANES, rows), _F32)]
    if emit_state:
        out_specs += [pl.BlockSpec(s, lambda b, t, n=len(s): (0,) * n) for s in state_shapes]
        out_shape += [jax.ShapeDtypeStruct(s, _F32) for s in state_shapes]
    scratch = [
        pltpu.VMEM((M_HEADS, M_DQK, M_DV), _F32),
        pltpu.VMEM((SUBLANES, M_DQK), _F32),
        pltpu.VMEM((SUBLANES, LANES), _F32),
        pltpu.VMEM((rows + SUBLANES, D_MODEL), _F32),
        pltpu.VMEM((rows, D_MODEL), _F32),
    ]
    outs = pl.pallas_call(
        functools.partial(_mixer_kernel, rows=rows, chunk=chunk, emit_state=emit_state),
        grid=(seqs, steps),
        in_specs=in_specs,
        out_specs=out_specs,
        out_shape=out_shape,
        scratch_shapes=scratch,
        compiler_params=pltpu.CompilerParams(
            dimension_semantics=("arbitrary", "arbitrary"), vmem_limit_bytes=VMEM_LIMIT),
        name="mixer",
    )(h, lw["norm_mix"], lw["w_main"], lw["w_if"], lw["b_if"], lw["conv_w"], lw["mh_gain"], lw["w_out"],
      lw["norm_ffn"], lw["w_route_t"], lw["b_route_t"], *state)
    return outs[0], outs[1], tuple(outs[2:])


def _expert_kernel(elo_ref, ehi_ref, nv_ref, nused_ref, inv_hbm, hx_hbm, nffn_ref, nfin_ref,
                   wg_lo, wg_hi, wu_lo, wu_hi, wd_lo, wd_hi, out_hbm,
                   idx_smem, xbuf, obuf, sem_i, sem_g, sem_s, *, tile, final_norm):
    j = pl.program_id(0)
    n_used = nused_ref[0]
    gather = RowGather(inv_hbm, hx_hbm, idx_smem, xbuf, sem_i, sem_g, tile)

    def scatter_wait(s):
        @pl.when(s >= 0)
        def _():
            rows = obuf.at[s % 2, pl.ds(0, nv_ref[jnp.maximum(s, 0)])]
            pltpu.make_async_copy(rows, rows, sem_s.at[s % 2]).wait()

    @pl.when(j < n_used)
    def _():
        slot = gather.begin(j)
        x = xbuf[slot].reshape(tile, ROW_W)
        hrow = x[:, 0:D_MODEL]
        w_lo = x[:, D_MODEL + 1:D_MODEL + 2]
        w_hi = x[:, D_MODEL + 2:D_MODEL + 3]
        xn = _rms(hrow, nffn_ref[...]).astype(_BF16)
        hid_lo = jax.nn.silu(_dot(xn, wg_lo[0])) * _dot(xn, wu_lo[0]) * w_lo
        hid_hi = jax.nn.silu(_dot(xn, wg_hi[0])) * _dot(xn, wu_hi[0]) * w_hi
        y = hrow + _dot(hid_lo.astype(_BF16), wd_lo[0]) + _dot(hid_hi.astype(_BF16), wd_hi[0])
        if final_norm:
            y = _rms(y, nfin_ref[...])
        scatter_wait(j - 2)
        obuf[j % 2, :, 0, :] = y

        nv = nv_ref[j]
        full = nv // SUBLANES

        def scatter(phase):
            i_slot, o_slot = phase % IDX_SLOTS, phase % 2

            def send(i):
                pltpu.make_async_copy(obuf.at[o_slot, i], out_hbm.at[pl.ds(idx_smem[i_slot, i], 1)],
                                      sem_s.at[o_slot]).start()

            def send_group(g, carry):
                for k in range(SUBLANES):
                    send(g * SUBLANES + k)
                return carry

            def send_row(i, carry):
                send(i)
                return carry
            lax.fori_loop(0, full, send_group, 0)
            lax.fori_loop(full * SUBLANES, nv, send_row, 0)
        _by_phase(j, scatter)

    @pl.when(j == n_used)
    def _():
        gather.drain(j)
        scatter_wait(j - 2)
        scatter_wait(j - 1)


_PAIR_LO = np.array([0, 0, 0, 1, 1, 2], np.int32)
_PAIR_HI = np.array([1, 2, 3, 2, 3, 3], np.int32)


def _sort_plan(bucket, n_tokens, tile):
    n_tiles = -(-n_tokens // tile) + N_BUCKETS + 1
    ids = jnp.arange(N_BUCKETS, dtype=jnp.int32)
    order = jnp.argsort(bucket, stable=True).astype(jnp.int32)
    counts = jnp.sum((bucket[:, None] == ids[None, :]).astype(jnp.int32), axis=0)
    tiles = (counts + tile - 1) // tile
    tile_end = jnp.cumsum(tiles)
    tile_start = tile_end - tiles
    cnt_start = jnp.cumsum(counts) - counts
    n_used = tile_end[-1]
    j = jnp.arange(n_tiles, dtype=jnp.int32)
    jj = jnp.minimum(j, jnp.maximum(n_used - 1, 0))
    sel = (jj[:, None] >= tile_start[None, :]) & (jj[:, None] < tile_end[None, :])
    pick = lambda v: jnp.sum(jnp.where(sel, v[None, :], 0), axis=1)
    tb = pick(ids)
    first = pick(cnt_start) + (jj - pick(tile_start)) * tile
    nv = jnp.where(j < n_used, jnp.clip(pick(cnt_start + counts) - first, 0, tile), 0).astype(jnp.int32)
    k = jnp.arange(tile, dtype=jnp.int32)
    slot = jnp.clip(first[:, None] + k[None, :], 0, n_tokens - 1)
    spread = (j[:, None] * tile + k[None, :]) % n_tokens
    inv = jnp.where(k[None, :] < nv[:, None], order[slot], spread).reshape(-1).astype(jnp.int32)
    grp = tb // N_PAIRS
    pr = tb % N_PAIRS
    e_lo = (grp * EXP_PER_GROUP + jnp.asarray(_PAIR_LO)[pr]).astype(jnp.int32)
    e_hi = (grp * EXP_PER_GROUP + jnp.asarray(_PAIR_HI)[pr]).astype(jnp.int32)
    return e_lo, e_hi, nv, n_used.reshape(1).astype(jnp.int32), inv, n_tiles


def _experts(hx, info, lw, norm_final, final_norm):
    n_tokens = hx.shape[0]
    tile = min(EXPERT_ROWS, n_tokens)
    bucket = info[:, 0, :].reshape(n_tokens).astype(jnp.int32)
    e_lo, e_hi, nv, n_used, inv, n_tiles = _sort_plan(bucket, n_tokens, tile)
    inv, idx_rows = _idx_tiles(inv, tile)
    any_spec = pl.BlockSpec(memory_space=pl.ANY)
    vec = pl.BlockSpec((1, D_MODEL), lambda j, lo, hi, nv, nu: (0, 0))
    w_in_lo = pl.BlockSpec((1, D_MODEL, D_EXPERT), lambda j, lo, hi, nv, nu: (lo[j], 0, 0))
    w_in_hi = pl.BlockSpec((1, D_MODEL, D_EXPERT), lambda j, lo, hi, nv, nu: (hi[j], 0, 0))
    w_dn_lo = pl.BlockSpec((1, D_EXPERT, D_MODEL), lambda j, lo, hi, nv, nu: (lo[j], 0, 0))
    w_dn_hi = pl.BlockSpec((1, D_EXPERT, D_MODEL), lambda j, lo, hi, nv, nu: (hi[j], 0, 0))
    return pl.pallas_call(
        functools.partial(_expert_kernel, tile=tile, final_norm=final_norm),
        grid_spec=pltpu.PrefetchScalarGridSpec(
            num_scalar_prefetch=4,
            grid=(n_tiles,),
            in_specs=[any_spec, any_spec, vec, vec, w_in_lo, w_in_hi, w_in_lo, w_in_hi, w_dn_lo, w_dn_hi],
            out_specs=any_spec,
            scratch_shapes=[
                pltpu.SMEM((IDX_SLOTS, idx_rows), jnp.int32),
                _row_buffer(tile, ROW_W),
                pltpu.VMEM((2, tile, 1, D_MODEL), _F32),
                pltpu.SemaphoreType.DMA((IDX_SLOTS,)),
                pltpu.SemaphoreType.DMA((2,)),
                pltpu.SemaphoreType.DMA((2,)),
            ],
        ),
        out_shape=jax.ShapeDtypeStruct((n_tokens, D_MODEL), _F32),
        compiler_params=pltpu.CompilerParams(
            dimension_semantics=("arbitrary",), vmem_limit_bytes=VMEM_LIMIT),
        name="experts",
    )(e_lo, e_hi, nv, n_used, inv, hx, lw["norm_ffn"], norm_final,
      lw["w_gate"], lw["w_gate"], lw["w_up"], lw["w_up"], lw["w_down"], lw["w_down"])


def _layer_weights(l, norm_mix, w_in, b_if, conv_w, mh_gain, w_out, norm_ffn,
                   w_group, b_group, w_router, b_router, w_gate, w_up, w_down):
    w = w_in[l]
    o_if = 2 * QK_W + V_W
    o_rest = o_if + 2 * M_HEADS
    w_main = jnp.concatenate([w[:, :o_if], w[:, o_rest:]], axis=1).astype(_BF16)
    w_if = jnp.pad(w[:, o_if:o_rest], ((0, 0), (0, LANES - 2 * M_HEADS))).astype(_BF16)
    pad_lanes = lambda a: jnp.pad(a, ((0, 0), (0, LANES - a.shape[1])))
    n_logits = N_GROUPS + N_GROUPS * EXP_PER_GROUP
    w_route = jnp.pad(jnp.concatenate([w_group[l], w_router[l]], axis=1).T, ((0, ROUTE_ROWS - n_logits), (0, 0)))
    w_route_hi = w_route.astype(_BF16)
    w_route_lo = (w_route - w_route_hi.astype(_F32)).astype(_BF16)
    b_route = jnp.pad(jnp.concatenate([b_group[l], b_router[l]]), (0, ROUTE_ROWS - n_logits))
    return {
        "norm_mix": norm_mix[l][None], "w_main": w_main, "w_if": w_if,
        "b_if": pad_lanes(b_if[l][None]), "conv_w": conv_w[l], "mh_gain": mh_gain[l][None],
        "w_out": w_out[l].astype(_BF16), "norm_ffn": norm_ffn[l][None],
        "w_route_t": jnp.concatenate([w_route_hi, w_route_lo], axis=0), "b_route_t": b_route[:, None],
        "w_gate": w_gate[l].astype(_BF16), "w_up": w_up[l].astype(_BF16), "w_down": w_down[l].astype(_BF16),
    }


def kernel(x, meta_tokens, norm_mix, w_in, b_if, conv_w, mh_gain, w_out, norm_ffn, w_group, b_group,
           w_router, b_router, w_gate, w_up, w_down, norm_final):
    batch, seq, d = x.shape
    depth = w_in.shape[0]
    assert d == D_MODEL and seq % MIX_ROWS == 0 and meta_tokens.shape == (N_META, D_MODEL)
    layers = [_layer_weights(l, norm_mix, w_in, b_if, conv_w, mh_gain, w_out, norm_ffn,
                             w_group, b_group, w_router, b_router, w_gate, w_up, w_down)
              for l in range(depth)]
    zero_state = (jnp.zeros((M_HEADS, M_DQK, M_DV), _F32), jnp.zeros((SUBLANES, M_DQK), _F32),
                  jnp.zeros((SUBLANES, LANES), _F32), jnp.zeros((SUBLANES, D_MODEL), _F32))

    hm = jnp.concatenate([jnp.zeros((REF_CHUNK - N_META, D_MODEL), _F32), meta_tokens.astype(_F32)], axis=0)
    nfin = norm_final[None]
    states = []
    for l in range(depth):
        hmx, info, st = _mixer(hm, 1, REF_CHUNK, layers[l], zero_state, emit_state=True)
        states.append(st)
        if l + 1 < depth:
            hm = _experts(hmx, info, layers[l], nfin, False)

    h = x.reshape(batch * seq, D_MODEL)
    for l in range(depth):
        hx, info, _ = _mixer(h, batch, seq, layers[l], states[l])
        h = _experts(hx, info, layers[l], nfin, l + 1 == depth)
    return h.reshape(batch, seq, D_MODEL)
```

```python
import functools

import jax
import jax.numpy as jnp
import numpy as np
from jax import lax
from jax.experimental import pallas as pl
from jax.experimental.pallas import tpu as pltpu

D_MODEL = 1024
N_META = 16
M_HEADS = 4
M_DQK = 128
M_DV = 256
REF_CHUNK = 64
N_GROUPS = 4
EXP_PER_GROUP = 4
N_PAIRS = 6
N_BUCKETS = N_GROUPS * N_PAIRS
D_EXPERT = 256
EPS = 1e-6
QK_W = M_HEADS * M_DQK
V_W = M_HEADS * M_DV

LANES = 128
SUBLANES = 8
ROW_W = D_MODEL + LANES
ROUTE_ROWS = 32
VMEM_LIMIT = 56 * 1024 * 1024

_OFF_Q = 0
_OFF_K = _OFF_Q + QK_W
_OFF_V = _OFF_K + QK_W
_OFF_GM = _OFF_V + V_W
_OFF_CX = _OFF_GM + V_W
_OFF_CB = _OFF_CX + D_MODEL
_OFF_CC = _OFF_CB + D_MODEL
_OFF_GC = _OFF_CC + D_MODEL
MAIN_W = _OFF_GC + D_MODEL

MIX_CHUNK = 256
MIX_ROWS = 512
EXPERT_ROWS = 256

_F32 = jnp.float32
_BF16 = jnp.bfloat16


def _rms(x, g):
    return x * lax.rsqrt(jnp.mean(x * x, axis=-1, keepdims=True) + EPS) * g


def _dot(a, b):
    return jnp.dot(a, b, preferred_element_type=_F32)


def _split2(x):
    hi = x.astype(_BF16)
    return hi, (x - hi.astype(_F32)).astype(_BF16)


MIN_IDX_TILE = 128


LOOKAHEAD_TILES = 2
IDX_SLOTS = 3


def _idx_tiles(idx, rows):
    idx_rows = max(rows, MIN_IDX_TILE)
    tiles = jnp.pad(idx.reshape(-1, rows), ((0, 0), (0, idx_rows - rows)))
    wrap = tiles[jnp.arange(LOOKAHEAD_TILES) % tiles.shape[0]]
    return jnp.concatenate([tiles, wrap], axis=0).reshape(-1), idx_rows


def _row_buffer(rows, width):
    return pltpu.VMEM((2, rows // SUBLANES, SUBLANES, width), _F32)


PHASES = 2 * IDX_SLOTS


def _by_phase(step, fn):
    for phase in range(PHASES):
        pl.when(step % PHASES == phase)(functools.partial(fn, phase))


class RowGather:
    def __init__(self, idx_hbm, src_hbm, idx_smem, buf, sem_i, sem_g, rows):
        self.idx_hbm, self.src_hbm, self.idx_smem, self.buf = idx_hbm, src_hbm, idx_smem, buf
        self.sem_i, self.sem_g, self.rows = sem_i, sem_g, rows
        self.idx_rows = idx_smem.shape[1]

    def _idx_copy(self, s):
        return pltpu.make_async_copy(self.idx_hbm.at[pl.ds(s * self.idx_rows, self.idx_rows)],
                                     self.idx_smem.at[s % IDX_SLOTS], self.sem_i.at[s % IDX_SLOTS])

    def _issue_phase(self, phase):
        i_slot, b_slot = phase % IDX_SLOTS, phase % 2
        for i in range(self.rows):
            pltpu.make_async_copy(self.src_hbm.at[self.idx_smem[i_slot, i]],
                                  self.buf.at[b_slot, i // SUBLANES, pl.ds(i % SUBLANES, 1)],
                                  self.sem_g.at[b_slot]).start()

    def _issue(self, s):
        _by_phase(s, self._issue_phase)

    def _wait_rows(self, s):
        pltpu.make_async_copy(self.buf.at[s % 2], self.buf.at[s % 2], self.sem_g.at[s % 2]).wait()

    def begin(self, step):
        @pl.when(step == 0)
        def _():
            self._idx_copy(step).start()
            self._idx_copy(step).wait()
            self._issue_phase(0)
            self._idx_copy(step + 1).start()

        self._idx_copy(step + 1).wait()
        self._issue(step + 1)
        self._idx_copy(step + 2).start()
        self._wait_rows(step)
        return step % 2

    def drain(self, step):
        self._wait_rows(step)
        self._idx_copy(step + 1).wait()


def _route(lgt):
    g = [lgt[i:i + 1, :] for i in range(N_GROUPS)]
    gmax = jnp.maximum(jnp.maximum(g[0], g[1]), jnp.maximum(g[2], g[3]))
    gsum = sum(jnp.exp(gi - gmax) for gi in g)
    g_val = 1.0 / gsum
    g_idx = jnp.where(g[0] == gmax, 0, jnp.where(g[1] == gmax, 1, jnp.where(g[2] == gmax, 2, 3)))
    e = []
    for k in range(EXP_PER_GROUP):
        col = lambda i, k=k: lgt[N_GROUPS + EXP_PER_GROUP * i + k:N_GROUPS + EXP_PER_GROUP * i + k + 1, :]
        e.append(jnp.where(g_idx == 0, col(0), jnp.where(g_idx == 1, col(1), jnp.where(g_idx == 2, col(2), col(3)))))
    v1 = jnp.maximum(jnp.maximum(e[0], e[1]), jnp.maximum(e[2], e[3]))
    i1 = jnp.where(e[0] == v1, 0, jnp.where(e[1] == v1, 1, jnp.where(e[2] == v1, 2, 3)))
    neg = jnp.float32(-jnp.inf)
    r = [jnp.where(i1 == k, neg, e[k]) for k in range(EXP_PER_GROUP)]
    v2 = jnp.maximum(jnp.maximum(r[0], r[1]), jnp.maximum(r[2], r[3]))
    i2 = jnp.where((r[0] == v2) & (i1 != 0), 0,
                   jnp.where((r[1] == v2) & (i1 != 1), 1, jnp.where((r[2] == v2) & (i1 != 2), 2, 3)))
    ex = jnp.exp(v2 - v1)
    w1 = g_val / (1.0 + ex)
    w2 = g_val * ex / (1.0 + ex)
    lo = jnp.minimum(i1, i2)
    hi = jnp.maximum(i1, i2)
    w_lo = jnp.where(i1 < i2, w1, w2)
    w_hi = jnp.where(i1 < i2, w2, w1)
    pair = jnp.where(lo == 0, hi - 1, jnp.where(lo == 1, hi + 1, 5))
    bucket = (g_idx * N_PAIRS + pair).astype(_F32)
    row = lax.broadcasted_iota(jnp.int32, (SUBLANES, lgt.shape[1]), 0)
    return jnp.where(row == 0, bucket, jnp.where(row == 1, w_lo, jnp.where(row == 2, w_hi, 0.0)))


def _mixer_kernel(*refs, rows, chunk, emit_state):
    refs = list(refs)
    take = lambda n: [refs.pop(0) for _ in range(n)]
    (h_ref, nmix_ref, wmain_ref, wif_ref, bif_ref, convw_ref, gain_ref, wout_ref, nffn_ref, wrt_ref,
     brt_ref, c0_ref, n0_ref, m0_ref, u0_ref) = take(15)
    out_ref, info_ref = take(2)
    if emit_state:
        cN_ref, nN_ref, mN_ref, uN_ref = take(4)
    c_scr, n_scr, m_scr, ubuf, ybuf = take(5)
    t = pl.program_id(1)

    @pl.when(t == 0)
    def _():
        c_scr[...] = c0_ref[...]
        n_scr[...] = n0_ref[...]
        m_scr[...] = m0_ref[...]
        ubuf[0:SUBLANES, :] = u0_ref[...]

    h = h_ref[...]
    xn = _rms(h, nmix_ref[...]).astype(_BF16)

    def proj(off, width):
        return _dot(xn, wmain_ref[:, off:off + width])

    u = proj(_OFF_CC, D_MODEL) * proj(_OFF_CX, D_MODEL)
    ubuf[SUBLANES:SUBLANES + rows, :] = u
    conv = (convw_ref[0:1, :] * ubuf[SUBLANES - 2:SUBLANES - 2 + rows, :]
            + convw_ref[1:2, :] * ubuf[SUBLANES - 1:SUBLANES - 1 + rows, :]
            + convw_ref[2:3, :] * u)
    ubuf[0:SUBLANES, :] = ubuf[rows:rows + SUBLANES, :]
    ybuf[...] = jax.nn.sigmoid(proj(_OFF_GC, D_MODEL)) * (proj(_OFF_CB, D_MODEL) * conv)

    gates = _dot(xn, wif_ref[...]) + bif_ref[...]
    lane = lax.broadcasted_iota(jnp.int32, gates.shape, 1)
    is_f = (lane >= M_HEADS) & (lane < 2 * M_HEADS)
    logf = jnp.where(is_f, jax.nn.log_sigmoid(gates), 0.0)
    ri = lax.broadcasted_iota(jnp.int32, (rows, rows), 0)
    ci = lax.broadcasted_iota(jnp.int32, (rows, rows), 1)
    shift = chunk.bit_length() - 1
    tri = (((ri >> shift) == (ci >> shift)) & (ci <= ri)).astype(_BF16)
    f_hi = logf.astype(_BF16)
    f_r = logf - f_hi.astype(_F32)
    f_mid = f_r.astype(_BF16)
    f_lo = (f_r - f_mid.astype(_F32)).astype(_BF16)
    bcum = _dot(tri, f_hi) + _dot(tri, f_mid) + _dot(tri, f_lo)
    gates_t = gates.T
    bcum_t = bcum.T

    q_all = (proj(_OFF_Q, QK_W) * (M_DQK ** -0.5)).astype(_BF16)
    k_all = proj(_OFF_K, QK_W)
    v_all = proj(_OFF_V, V_W).astype(_BF16)
    gm_all = proj(_OFF_GM, V_W)

    ti = lax.broadcasted_iota(jnp.int32, (chunk, chunk), 0)
    si = lax.broadcasted_iota(jnp.int32, (chunk, chunk), 1)
    causal = si <= ti

    for c in range(rows // chunk):
        r0 = c * chunk
        for hd in range(M_HEADS):
            q = q_all[r0:r0 + chunk, hd * M_DQK:(hd + 1) * M_DQK]
            kf = k_all[r0:r0 + chunk, hd * M_DQK:(hd + 1) * M_DQK]
            v = v_all[r0:r0 + chunk, hd * M_DV:(hd + 1) * M_DV]
            b_c = bcum[r0:r0 + chunk, M_HEADS + hd:M_HEADS + hd + 1]
            i_c = gates[r0:r0 + chunk, hd:hd + 1]
            b_r = bcum_t[M_HEADS + hd:M_HEADS + hd + 1, r0:r0 + chunk]
            i_r = gates_t[hd:hd + 1, r0:r0 + chunk]
            m_prev = m_scr[hd:hd + 1, 0:1]
            c_prev = c_scr[hd]
            n_prev = n_scr[hd:hd + 1, :]

            log_d = jnp.where(causal, b_c - b_r + i_r, -jnp.inf)
            inter = b_c + m_prev
            m_t = jnp.maximum(inter, jnp.max(log_d, axis=-1, keepdims=True))
            dmat = jnp.exp(log_d - m_t)
            s = lax.dot_general(q, kf.astype(_BF16), (((1,), (1,)), ((), ())),
                                preferred_element_type=_F32) * dmat
            e_inter = jnp.exp(inter - m_t)
            num = _dot(s.astype(_BF16), v) + e_inter * _dot(q, c_prev.astype(_BF16))
            den = (jnp.sum(s, axis=-1, keepdims=True)
                   + e_inter * jnp.sum(q.astype(_F32) * n_prev, axis=-1, keepdims=True))
            hc = num / jnp.maximum(jnp.abs(den), jnp.exp(-m_t))

            b_last = b_c[chunk - 1:chunk, :]
            log_w = b_last - b_c + i_c
            m_new = jnp.maximum(b_last + m_prev, jnp.max(log_w, axis=0, keepdims=True))
            kw = kf * jnp.exp(log_w - m_new)
            decay = jnp.exp(b_last + m_prev - m_new)
            c_scr[hd] = decay * c_prev + lax.dot_general(
                kw.astype(_BF16), v, (((0,), (0,)), ((), ())), preferred_element_type=_F32)
            n_scr[hd:hd + 1, :] = decay * n_prev + jnp.sum(kw, axis=0, keepdims=True)
            m_scr[hd:hd + 1, :] = jnp.broadcast_to(m_new, (1, LANES))

            hn = hc * lax.rsqrt(jnp.mean(hc * hc, axis=-1, keepdims=True) + EPS)
            cols = slice(hd * M_DV, (hd + 1) * M_DV)
            ybuf[r0:r0 + chunk, cols] += (jax.nn.sigmoid(gm_all[r0:r0 + chunk, cols])
                                          * (hn * gain_ref[:, cols]))

    h1 = h + _dot(ybuf[...].astype(_BF16), wout_ref[...])

    x_hi, x_lo = _split2(_rms(h1, nffn_ref[...]))
    nt = (((1,), (1,)), ((), ()))
    hi_all = lax.dot_general(wrt_ref[...], x_hi, nt, preferred_element_type=_F32)
    lo_x = lax.dot_general(wrt_ref[0:ROUTE_ROWS, :], x_lo, nt, preferred_element_type=_F32)
    lgt = hi_all[0:ROUTE_ROWS] + hi_all[ROUTE_ROWS:2 * ROUTE_ROWS] + lo_x + brt_ref[...]
    info_t = _route(lgt)
    info_ref[0] = info_t
    out_ref[:, 0, 0:D_MODEL] = h1
    out_ref[:, 0, D_MODEL:ROW_W] = jnp.concatenate(
        [info_t, jnp.zeros((LANES - SUBLANES, rows), _F32)], axis=0).T

    if emit_state:
        @pl.when(t == pl.num_programs(1) - 1)
        def _():
            cN_ref[...] = c_scr[...]
            nN_ref[...] = n_scr[...]
            mN_ref[...] = m_scr[...]
            uN_ref[...] = ubuf[0:SUBLANES, :]


def _mixer(h, seqs, seq_len, lw, state, emit_state=False):
    rows = min(MIX_ROWS, seq_len)
    chunk = min(MIX_CHUNK, rows)
    steps = seq_len // rows
    const = lambda shape: pl.BlockSpec(shape, lambda b, t: (0,) * len(shape), pipeline_mode=pl.Buffered(1))
    state_shapes = [(M_HEADS, M_DQK, M_DV), (SUBLANES, M_DQK), (SUBLANES, LANES), (SUBLANES, D_MODEL)]
    in_specs = [
        pl.BlockSpec((rows, D_MODEL), lambda b, t: (b * steps + t, 0)),
        const((1, D_MODEL)), const((D_MODEL, MAIN_W)), const((D_MODEL, LANES)), const((1, LANES)),
        const((3, D_MODEL)), const((1, V_W)), const((D_MODEL, D_MODEL)), const((1, D_MODEL)),
        const((2 * ROUTE_ROWS, D_MODEL)), const((ROUTE_ROWS, 1)),
    ] + [const(s) for s in state_shapes]
    out_specs = [pl.BlockSpec((rows, 1, ROW_W), lambda b, t: (b * steps + t, 0, 0)),
                 pl.BlockSpec((1, SUBLANES, rows), lambda b, t: (b * steps + t, 0, 0))]
    out_shape = [jax.ShapeDtypeStruct((seqs * seq_len, 1, ROW_W), _F32),
                 jax.ShapeDtypeStruct((seqs * steps, SUBLANES, rows), _F32)]
    if emit_state:
        out_specs += [pl.BlockSpec(s, lambda b, t, n=len(s): (0,) * n) for s in state_shapes]
        out_shape += [jax.ShapeDtypeStruct(s, _F32) for s in state_shapes]
    scratch = [
        pltpu.VMEM((M_HEADS, M_DQK, M_DV), _F32),
        pltpu.VMEM((SUBLANES, M_DQK), _F32),
        pltpu.VMEM((SUBLANES, LANES), _F32),
        pltpu.VMEM((rows + SUBLANES, D_MODEL), _F32),
        pltpu.VMEM((rows, D_MODEL), _F32),
    ]
    outs = pl.pallas_call(
        functools.partial(_mixer_kernel, rows=rows, chunk=chunk, emit_state=emit_state),
        grid=(seqs, steps),
        in_specs=in_specs,
        out_specs=out_specs,
        out_shape=out_shape,
        scratch_shapes=scratch,
        compiler_params=pltpu.CompilerParams(
            dimension_semantics=("arbitrary", "arbitrary"), vmem_limit_bytes=VMEM_LIMIT),
        name="mixer",
    )(h, lw["norm_mix"], lw["w_main"], lw["w_if"], lw["b_if"], lw["conv_w"], lw["mh_gain"], lw["w_out"],
      lw["norm_ffn"], lw["w_route_t"], lw["b_route_t"], *state)
    return outs[0], outs[1], tuple(outs[2:])


def _expert_kernel(elo_ref, ehi_ref, nv_ref, nused_ref, inv_hbm, hx_hbm, nffn_ref, nfin_ref,
                   wg_lo, wg_hi, wu_lo, wu_hi, wd_lo, wd_hi, out_hbm,
                   idx_smem, xbuf, obuf, sem_i, sem_g, sem_s, *, tile, final_norm):
    j = pl.program_id(0)
    n_used = nused_ref[0]
    gather = RowGather(inv_hbm, hx_hbm, idx_smem, xbuf, sem_i, sem_g, tile)

    def scatter_wait(s):
        @pl.when(s >= 0)
        def _():
            rows = obuf.at[s % 2, pl.ds(0, nv_ref[jnp.maximum(s, 0)])]
            pltpu.make_async_copy(rows, rows, sem_s.at[s % 2]).wait()

    @pl.when(j < n_used)
    def _():
        slot = gather.begin(j)
        x = xbuf[slot].reshape(tile, ROW_W)
        hrow = x[:, 0:D_MODEL]
        w_lo = x[:, D_MODEL + 1:D_MODEL + 2]
        w_hi = x[:, D_MODEL + 2:D_MODEL + 3]
        xn = _rms(hrow, nffn_ref[...]).astype(_BF16)
        hid_lo = jax.nn.silu(_dot(xn, wg_lo[0])) * _dot(xn, wu_lo[0]) * w_lo
        hid_hi = jax.nn.silu(_dot(xn, wg_hi[0])) * _dot(xn, wu_hi[0]) * w_hi
        y = hrow + _dot(hid_lo.astype(_BF16), wd_lo[0]) + _dot(hid_hi.astype(_BF16), wd_hi[0])
        if final_norm:
            y = _rms(y, nfin_ref[...])
        scatter_wait(j - 2)
        obuf[j % 2, :, 0, :] = y

        nv = nv_ref[j]
        full = nv // SUBLANES

        def scatter(phase):
            i_slot, o_slot = phase % IDX_SLOTS, phase % 2

            def send(i):
                pltpu.make_async_copy(obuf.at[o_slot, i], out_hbm.at[pl.ds(idx_smem[i_slot, i], 1)],
                                      sem_s.at[o_slot]).start()

            def send_group(g):
                for k in range(SUBLANES):
                    send(g * SUBLANES + k)

            def send_row(i, carry):
                send(i)
                return carry
            for g in range(tile // SUBLANES):
                pl.when(g < full)(functools.partial(send_group, g))
            lax.fori_loop(full * SUBLANES, nv, send_row, 0)
        _by_phase(j, scatter)

    @pl.when(j == n_used)
    def _():
        gather.drain(j)
        scatter_wait(j - 2)
        scatter_wait(j - 1)


_PAIR_LO = np.array([0, 0, 0, 1, 1, 2], np.int32)
_PAIR_HI = np.array([1, 2, 3, 2, 3, 3], np.int32)


def _sort_plan(bucket, n_tokens, tile):
    n_tiles = -(-n_tokens // tile) + N_BUCKETS + 1
    ids = jnp.arange(N_BUCKETS, dtype=jnp.int32)
    order = jnp.argsort(bucket, stable=True).astype(jnp.int32)
    counts = jnp.sum((bucket[:, None] == ids[None, :]).astype(jnp.int32), axis=0)
    tiles = (counts + tile - 1) // tile
    tile_end = jnp.cumsum(tiles)
    tile_start = tile_end - tiles
    cnt_start = jnp.cumsum(counts) - counts
    n_used = tile_end[-1]
    j = jnp.arange(n_tiles, dtype=jnp.int32)
    jj = jnp.minimum(j, jnp.maximum(n_used - 1, 0))
    sel = (jj[:, None] >= tile_start[None, :]) & (jj[:, None] < tile_end[None, :])
    pick = lambda v: jnp.sum(jnp.where(sel, v[None, :], 0), axis=1)
    tb = pick(ids)
    first = pick(cnt_start) + (jj - pick(tile_start)) * tile
    nv = jnp.where(j < n_used, jnp.clip(pick(cnt_start + counts) - first, 0, tile), 0).astype(jnp.int32)
    k = jnp.arange(tile, dtype=jnp.int32)
    slot = jnp.clip(first[:, None] + k[None, :], 0, n_tokens - 1)
    spread = (j[:, None] * tile + k[None, :]) % n_tokens
    inv = jnp.where(k[None, :] < nv[:, None], order[slot], spread).reshape(-1).astype(jnp.int32)
    grp = tb // N_PAIRS
    pr = tb % N_PAIRS
    e_lo = (grp * EXP_PER_GROUP + jnp.asarray(_PAIR_LO)[pr]).astype(jnp.int32)
    e_hi = (grp * EXP_PER_GROUP + jnp.asarray(_PAIR_HI)[pr]).astype(jnp.int32)
    return e_lo, e_hi, nv, n_used.reshape(1).astype(jnp.int32), inv, n_tiles


def _experts(hx, info, lw, norm_final, final_norm):
    n_tokens = hx.shape[0]
    tile = min(EXPERT_ROWS, n_tokens)
    bucket = info[:, 0, :].reshape(n_tokens).astype(jnp.int32)
    e_lo, e_hi, nv, n_used, inv, n_tiles = _sort_plan(bucket, n_tokens, tile)
    inv, idx_rows = _idx_tiles(inv, tile)
    any_spec = pl.BlockSpec(memory_space=pl.ANY)
    vec = pl.BlockSpec((1, D_MODEL), lambda j, lo, hi, nv, nu: (0, 0))
    w_in_lo = pl.BlockSpec((1, D_MODEL, D_EXPERT), lambda j, lo, hi, nv, nu: (lo[j], 0, 0))
    w_in_hi = pl.BlockSpec((1, D_MODEL, D_EXPERT), lambda j, lo, hi, nv, nu: (hi[j], 0, 0))
    w_dn_lo = pl.BlockSpec((1, D_EXPERT, D_MODEL), lambda j, lo, hi, nv, nu: (lo[j], 0, 0))
    w_dn_hi = pl.BlockSpec((1, D_EXPERT, D_MODEL), lambda j, lo, hi, nv, nu: (hi[j], 0, 0))
    return pl.pallas_call(
        functools.partial(_expert_kernel, tile=tile, final_norm=final_norm),
        grid_spec=pltpu.PrefetchScalarGridSpec(
            num_scalar_prefetch=4,
            grid=(n_tiles,),
            in_specs=[any_spec, any_spec, vec, vec, w_in_lo, w_in_hi, w_in_lo, w_in_hi, w_dn_lo, w_dn_hi],
            out_specs=any_spec,
            scratch_shapes=[
                pltpu.SMEM((IDX_SLOTS, idx_rows), jnp.int32),
                _row_buffer(tile, ROW_W),
                pltpu.VMEM((2, tile, 1, D_MODEL), _F32),
                pltpu.SemaphoreType.DMA((IDX_SLOTS,)),
                pltpu.SemaphoreType.DMA((2,)),
                pltpu.SemaphoreType.DMA((2,)),
            ],
        ),
        out_shape=jax.ShapeDtypeStruct((n_tokens, D_MODEL), _F32),
        compiler_params=pltpu.CompilerParams(
            dimension_semantics=("arbitrary",), vmem_limit_bytes=VMEM_LIMIT),
        name="experts",
    )(e_lo, e_hi, nv, n_used, inv, hx, lw["norm_ffn"], norm_final,
      lw["w_gate"], lw["w_gate"], lw["w_up"], lw["w_up"], lw["w_down"], lw["w_down"])


def _layer_weights(l, norm_mix, w_in, b_if, conv_w, mh_gain, w_out, norm_ffn,
                   w_group, b_group, w_router, b_router, w_gate, w_up, w_down):
    w = w_in[l]
    o_if = 2 * QK_W + V_W
    o_rest = o_if + 2 * M_HEADS
    w_main = jnp.concatenate([w[:, :o_if], w[:, o_rest:]], axis=1).astype(_BF16)
    w_if = jnp.pad(w[:, o_if:o_rest], ((0, 0), (0, LANES - 2 * M_HEADS))).astype(_BF16)
    pad_lanes = lambda a: jnp.pad(a, ((0, 0), (0, LANES - a.shape[1])))
    n_logits = N_GROUPS + N_GROUPS * EXP_PER_GROUP
    w_route = jnp.pad(jnp.concatenate([w_group[l], w_router[l]], axis=1).T, ((0, ROUTE_ROWS - n_logits), (0, 0)))
    w_route_hi = w_route.astype(_BF16)
    w_route_lo = (w_route - w_route_hi.astype(_F32)).astype(_BF16)
    b_route = jnp.pad(jnp.concatenate([b_group[l], b_router[l]]), (0, ROUTE_ROWS - n_logits))
    return {
        "norm_mix": norm_mix[l][None], "w_main": w_main, "w_if": w_if,
        "b_if": pad_lanes(b_if[l][None]), "conv_w": conv_w[l], "mh_gain": mh_gain[l][None],
        "w_out": w_out[l].astype(_BF16), "norm_ffn": norm_ffn[l][None],
        "w_route_t": jnp.concatenate([w_route_hi, w_route_lo], axis=0), "b_route_t": b_route[:, None],
        "w_gate": w_gate[l].astype(_BF16), "w_up": w_up[l].astype(_BF16), "w_down": w_down[l].astype(_BF16),
    }


def kernel(x, meta_tokens, norm_mix, w_in, b_if, conv_w, mh_gain, w_out, norm_ffn, w_group, b_group,
           w_router, b_router, w_gate, w_up, w_down, norm_final):
    batch, seq, d = x.shape
    depth = w_in.shape[0]
    assert d == D_MODEL and seq % MIX_ROWS == 0 and meta_tokens.shape == (N_META, D_MODEL)
    layers = [_layer_weights(l, norm_mix, w_in, b_if, conv_w, mh_gain, w_out, norm_ffn,
                             w_group, b_group, w_router, b_router, w_gate, w_up, w_down)
              for l in range(depth)]
    zero_state = (jnp.zeros((M_HEADS, M_DQK, M_DV), _F32), jnp.zeros((SUBLANES, M_DQK), _F32),
                  jnp.zeros((SUBLANES, LANES), _F32), jnp.zeros((SUBLANES, D_MODEL), _F32))

    hm = jnp.concatenate([jnp.zeros((REF_CHUNK - N_META, D_MODEL), _F32), meta_tokens.astype(_F32)], axis=0)
    nfin = norm_final[None]
    states = []
    for l in range(depth):
        hmx, info, st = _mixer(hm, 1, REF_CHUNK, layers[l], zero_state, emit_state=True)
        states.append(st)
        if l + 1 < depth:
            hm = _experts(hmx, info, layers[l], nfin, False)

    h = x.reshape(batch * seq, D_MODEL)
    for l in range(depth):
        hx, info, _ = _mixer(h, batch, seq, layers[l], states[l])
        h = _experts(hx, info, layers[l], nfin, l + 1 == depth)
    return h.reshape(batch, seq, D_MODEL)
```

```python
import functools

import jax
import jax.numpy as jnp
import numpy as np
from jax import lax
from jax.experimental import pallas as pl
from jax.experimental.pallas import tpu as pltpu

D_MODEL = 1024
N_META = 16
M_HEADS = 4
M_DQK = 128
M_DV = 256
REF_CHUNK = 64
N_GROUPS = 4
EXP_PER_GROUP = 4
N_PAIRS = 6
N_BUCKETS = N_GROUPS * N_PAIRS
D_EXPERT = 256
EPS = 1e-6
QK_W = M_HEADS * M_DQK
V_W = M_HEADS * M_DV

LANES = 128
SUBLANES = 8
ROW_W = D_MODEL + LANES
ROUTE_ROWS = 32
VMEM_LIMIT = 56 * 1024 * 1024

_OFF_Q = 0
_OFF_K = _OFF_Q + QK_W
_OFF_V = _OFF_K + QK_W
_OFF_GM = _OFF_V + V_W
_OFF_CX = _OFF_GM + V_W
_OFF_CB = _OFF_CX + D_MODEL
_OFF_CC = _OFF_CB + D_MODEL
_OFF_GC = _OFF_CC + D_MODEL
MAIN_W = _OFF_GC + D_MODEL

MIX_CHUNK = 256
MIX_ROWS = 512
EXPERT_ROWS = 256

_F32 = jnp.float32
_BF16 = jnp.bfloat16


def _rms(x, g):
    return x * lax.rsqrt(jnp.mean(x * x, axis=-1, keepdims=True) + EPS) * g


def _dot(a, b):
    return jnp.dot(a, b, preferred_element_type=_F32)


def _split2(x):
    hi = x.astype(_BF16)
    return hi, (x - hi.astype(_F32)).astype(_BF16)


MIN_IDX_TILE = 128


LOOKAHEAD_TILES = 2
IDX_SLOTS = 4


def _idx_tiles(idx, rows):
    idx_rows = max(rows, MIN_IDX_TILE)
    if idx_rows == rows:
        return idx, idx_rows
    return jnp.pad(idx.reshape(-1, rows), ((0, 0), (0, idx_rows - rows))).reshape(-1), idx_rows


def _row_buffer(rows, width):
    return pltpu.VMEM((2, rows // SUBLANES, SUBLANES, width), _F32)


PHASES = IDX_SLOTS


def _by_phase(step, fn):
    for phase in range(PHASES):
        pl.when(step % PHASES == phase)(functools.partial(fn, phase))


class RowGather:
    def __init__(self, idx_hbm, src_hbm, idx_smem, buf, sem_i, sem_g, rows):
        self.idx_hbm, self.src_hbm, self.idx_smem, self.buf = idx_hbm, src_hbm, idx_smem, buf
        self.sem_i, self.sem_g, self.rows = sem_i, sem_g, rows
        self.idx_rows = idx_smem.shape[1]

    def _idx_copy(self, s, slot):
        return pltpu.make_async_copy(self.idx_hbm.at[pl.ds(s * self.idx_rows, self.idx_rows)],
                                     self.idx_smem.at[slot], self.sem_i.at[slot])

    def _issue(self, phase):
        i_slot, b_slot = phase % IDX_SLOTS, phase % 2
        for i in range(self.rows):
            pltpu.make_async_copy(self.src_hbm.at[self.idx_smem[i_slot, i]],
                                  self.buf.at[b_slot, i // SUBLANES, pl.ds(i % SUBLANES, 1)],
                                  self.sem_g.at[b_slot]).start()

    def _wait_rows(self, slot):
        pltpu.make_async_copy(self.buf.at[slot], self.buf.at[slot], self.sem_g.at[slot]).wait()

    def first(self, step):
        @pl.when(step == 0)
        def _():
            self._idx_copy(step, 0).start()
            self._idx_copy(step, 0).wait()
            self._issue(0)
            self._idx_copy(step + 1, 1).start()

    def rows_ready(self, step, phase):
        self._idx_copy(step + 1, (phase + 1) % IDX_SLOTS).wait()
        self._wait_rows(phase % 2)
        return phase % 2

    def prefetch(self, step, phase):
        self._issue(phase + 1)
        self._idx_copy(step + 2, (phase + 2) % IDX_SLOTS).start()

    def drain(self, step):
        self._wait_rows(step % 2)
        self._idx_copy(step + 1, (step + 1) % IDX_SLOTS).wait()


def _route(lgt):
    g = [lgt[i:i + 1, :] for i in range(N_GROUPS)]
    gmax = jnp.maximum(jnp.maximum(g[0], g[1]), jnp.maximum(g[2], g[3]))
    gsum = sum(jnp.exp(gi - gmax) for gi in g)
    g_val = 1.0 / gsum
    g_idx = jnp.where(g[0] == gmax, 0, jnp.where(g[1] == gmax, 1, jnp.where(g[2] == gmax, 2, 3)))
    e = []
    for k in range(EXP_PER_GROUP):
        col = lambda i, k=k: lgt[N_GROUPS + EXP_PER_GROUP * i + k:N_GROUPS + EXP_PER_GROUP * i + k + 1, :]
        e.append(jnp.where(g_idx == 0, col(0), jnp.where(g_idx == 1, col(1), jnp.where(g_idx == 2, col(2), col(3)))))
    v1 = jnp.maximum(jnp.maximum(e[0], e[1]), jnp.maximum(e[2], e[3]))
    i1 = jnp.where(e[0] == v1, 0, jnp.where(e[1] == v1, 1, jnp.where(e[2] == v1, 2, 3)))
    neg = jnp.float32(-jnp.inf)
    r = [jnp.where(i1 == k, neg, e[k]) for k in range(EXP_PER_GROUP)]
    v2 = jnp.maximum(jnp.maximum(r[0], r[1]), jnp.maximum(r[2], r[3]))
    i2 = jnp.where((r[0] == v2) & (i1 != 0), 0,
                   jnp.where((r[1] == v2) & (i1 != 1), 1, jnp.where((r[2] == v2) & (i1 != 2), 2, 3)))
    ex = jnp.exp(v2 - v1)
    w1 = g_val / (1.0 + ex)
    w2 = g_val * ex / (1.0 + ex)
    lo = jnp.minimum(i1, i2)
    hi = jnp.maximum(i1, i2)
    w_lo = jnp.where(i1 < i2, w1, w2)
    w_hi = jnp.where(i1 < i2, w2, w1)
    pair = jnp.where(lo == 0, hi - 1, jnp.where(lo == 1, hi + 1, 5))
    bucket = (g_idx * N_PAIRS + pair).astype(_F32)
    row = lax.broadcasted_iota(jnp.int32, (SUBLANES, lgt.shape[1]), 0)
    return jnp.where(row == 0, bucket, jnp.where(row == 1, w_lo, jnp.where(row == 2, w_hi, 0.0)))


def _mixer_kernel(*refs, rows, chunk, emit_state):
    refs = list(refs)
    take = lambda n: [refs.pop(0) for _ in range(n)]
    (h_ref, nmix_ref, wmain_ref, wif_ref, bif_ref, convw_ref, gain_ref, wout_ref, nffn_ref, wrt_ref,
     brt_ref, tri_ref, c0_ref, n0_ref, m0_ref, u0_ref) = take(16)
    out_ref, bucket_ref = take(2)
    if emit_state:
        cN_ref, nN_ref, mN_ref, uN_ref = take(4)
    c_scr, n_scr, m_scr, ubuf, ybuf = take(5)
    t = pl.program_id(1)

    @pl.when(t == 0)
    def _():
        c_scr[...] = c0_ref[...]
        n_scr[...] = n0_ref[...]
        m_scr[...] = m0_ref[...]
        ubuf[0:SUBLANES, :] = u0_ref[...]

    h = h_ref[...]
    xn = _rms(h, nmix_ref[...]).astype(_BF16)

    def proj(off, width):
        return _dot(xn, wmain_ref[:, off:off + width])

    u = proj(_OFF_CC, D_MODEL) * proj(_OFF_CX, D_MODEL)
    ubuf[SUBLANES:SUBLANES + rows, :] = u
    conv = (convw_ref[0:1, :] * ubuf[SUBLANES - 2:SUBLANES - 2 + rows, :]
            + convw_ref[1:2, :] * ubuf[SUBLANES - 1:SUBLANES - 1 + rows, :]
            + convw_ref[2:3, :] * u)
    ubuf[0:SUBLANES, :] = ubuf[rows:rows + SUBLANES, :]
    ybuf[...] = jax.nn.sigmoid(proj(_OFF_GC, D_MODEL)) * (proj(_OFF_CB, D_MODEL) * conv)

    gates = _dot(xn, wif_ref[...]) + bif_ref[...]
    lane = lax.broadcasted_iota(jnp.int32, gates.shape, 1)
    is_f = (lane >= M_HEADS) & (lane < 2 * M_HEADS)
    logf = jnp.where(is_f, jax.nn.log_sigmoid(gates), 0.0)
    tri = tri_ref[...]
    f_hi = logf.astype(_BF16)
    f_r = logf - f_hi.astype(_F32)
    f_mid = f_r.astype(_BF16)
    f_lo = (f_r - f_mid.astype(_F32)).astype(_BF16)
    bcum = _dot(tri, f_hi) + _dot(tri, f_mid) + _dot(tri, f_lo)
    gates_t = gates.T
    bcum_t = bcum.T

    q_all = (proj(_OFF_Q, QK_W) * (M_DQK ** -0.5)).astype(_BF16)
    k_all = proj(_OFF_K, QK_W)
    v_all = proj(_OFF_V, V_W).astype(_BF16)
    gm_all = proj(_OFF_GM, V_W)

    ti = lax.broadcasted_iota(jnp.int32, (chunk, chunk), 0)
    si = lax.broadcasted_iota(jnp.int32, (chunk, chunk), 1)
    causal = si <= ti

    for c in range(rows // chunk):
        r0 = c * chunk
        for hd in range(M_HEADS):
            q = q_all[r0:r0 + chunk, hd * M_DQK:(hd + 1) * M_DQK]
            kf = k_all[r0:r0 + chunk, hd * M_DQK:(hd + 1) * M_DQK]
            v = v_all[r0:r0 + chunk, hd * M_DV:(hd + 1) * M_DV]
            b_c = bcum[r0:r0 + chunk, M_HEADS + hd:M_HEADS + hd + 1]
            i_c = gates[r0:r0 + chunk, hd:hd + 1]
            b_r = bcum_t[M_HEADS + hd:M_HEADS + hd + 1, r0:r0 + chunk]
            i_r = gates_t[hd:hd + 1, r0:r0 + chunk]
            m_prev = m_scr[hd:hd + 1, 0:1]
            c_prev = c_scr[hd]
            n_prev = n_scr[hd:hd + 1, :]

            log_d = jnp.where(causal, b_c - b_r + i_r, -jnp.inf)
            inter = b_c + m_prev
            m_t = jnp.maximum(inter, jnp.max(log_d, axis=-1, keepdims=True))
            dmat = jnp.exp(log_d - m_t)
            s = lax.dot_general(q, kf.astype(_BF16), (((1,), (1,)), ((), ())),
                                preferred_element_type=_F32) * dmat
            e_inter = jnp.exp(inter - m_t)
            num = _dot(s.astype(_BF16), v) + e_inter * _dot(q, c_prev.astype(_BF16))
            den = (jnp.sum(s, axis=-1, keepdims=True)
                   + e_inter * jnp.sum(q.astype(_F32) * n_prev, axis=-1, keepdims=True))
            hc = num / jnp.maximum(jnp.abs(den), jnp.exp(-m_t))

            b_last = b_c[chunk - 1:chunk, :]
            log_w = b_last - b_c + i_c
            m_new = jnp.maximum(b_last + m_prev, jnp.max(log_w, axis=0, keepdims=True))
            kw = kf * jnp.exp(log_w - m_new)
            decay = jnp.exp(b_last + m_prev - m_new)
            c_scr[hd] = decay * c_prev + lax.dot_general(
                kw.astype(_BF16), v, (((0,), (0,)), ((), ())), preferred_element_type=_F32)
            n_scr[hd:hd + 1, :] = decay * n_prev + jnp.sum(kw, axis=0, keepdims=True)
            m_scr[hd:hd + 1, :] = jnp.broadcast_to(m_new, (1, LANES))

            hn = hc * lax.rsqrt(jnp.mean(hc * hc, axis=-1, keepdims=True) + EPS)
            cols = slice(hd * M_DV, (hd + 1) * M_DV)
            ybuf[r0:r0 + chunk, cols] += (jax.nn.sigmoid(gm_all[r0:r0 + chunk, cols])
                                          * (hn * gain_ref[:, cols]))

    h1 = h + _dot(ybuf[...].astype(_BF16), wout_ref[...])

    x_hi, x_lo = _split2(_rms(h1, nffn_ref[...]))
    nt = (((1,), (1,)), ((), ()))
    hi_all = lax.dot_general(wrt_ref[...], x_hi, nt, preferred_element_type=_F32)
    lo_x = lax.dot_general(wrt_ref[0:ROUTE_ROWS, :], x_lo, nt, preferred_element_type=_F32)
    lgt = hi_all[0:ROUTE_ROWS] + hi_all[ROUTE_ROWS:2 * ROUTE_ROWS] + lo_x + brt_ref[...]
    info_t = _route(lgt)
    bucket_ref[0] = info_t[0:1, :].astype(jnp.int32)
    out_ref[:, 0, 0:D_MODEL] = h1
    out_ref[:, 0, D_MODEL:ROW_W] = jnp.concatenate(
        [info_t, jnp.zeros((LANES - SUBLANES, rows), _F32)], axis=0).T

    if emit_state:
        @pl.when(t == pl.num_programs(1) - 1)
        def _():
            cN_ref[...] = c_scr[...]
            nN_ref[...] = n_scr[...]
            mN_ref[...] = m_scr[...]
            uN_ref[...] = ubuf[0:SUBLANES, :]


def _mixer(h, seqs, seq_len, lw, state, emit_state=False):
    rows = min(MIX_ROWS, seq_len)
    chunk = min(MIX_CHUNK, rows)
    steps = seq_len // rows
    r = np.arange(rows)
    tri = jnp.asarray((r[:, None] // chunk == r[None, :] // chunk) & (r[None, :] <= r[:, None]), _BF16)
    const = lambda shape: pl.BlockSpec(shape, lambda b, t: (0,) * len(shape), pipeline_mode=pl.Buffered(1))
    state_shapes = [(M_HEADS, M_DQK, M_DV), (SUBLANES, M_DQK), (SUBLANES, LANES), (SUBLANES, D_MODEL)]
    in_specs = [
        pl.BlockSpec((rows, D_MODEL), lambda b, t: (b * steps + t, 0)),
        const((1, D_MODEL)), const((D_MODEL, MAIN_W)), const((D_MODEL, LANES)), const((1, LANES)),
        const((3, D_MODEL)), const((1, V_W)), const((D_MODEL, D_MODEL)), const((1, D_MODEL)),
        const((2 * ROUTE_ROWS, D_MODEL)), const((ROUTE_ROWS, 1)), const((rows, rows)),
    ] + [const(s) for s in state_shapes]
    out_specs = [pl.BlockSpec((rows, 1, ROW_W), lambda b, t: (b * steps + t, 0, 0)),
                 pl.BlockSpec((1, 1, rows), lambda b, t: (b * steps + t, 0, 0))]
    out_shape = [jax.ShapeDtypeStruct((seqs * seq_len, 1, ROW_W), _F32),
                 jax.ShapeDtypeStruct((seqs * steps, 1, rows), jnp.int32)]
    if emit_state:
        out_specs += [pl.BlockSpec(s, lambda b, t, n=len(s): (0,) * n) for s in state_shapes]
        out_shape += [jax.ShapeDtypeStruct(s, _F32) for s in state_shapes]
    scratch = [
        pltpu.VMEM((M_HEADS, M_DQK, M_DV), _F32),
        pltpu.VMEM((SUBLANES, M_DQK), _F32),
        pltpu.VMEM((SUBLANES, LANES), _F32),
        pltpu.VMEM((rows + SUBLANES, D_MODEL), _F32),
        pltpu.VMEM((rows, D_MODEL), _F32),
    ]
    outs = pl.pallas_call(
        functools.partial(_mixer_kernel, rows=rows, chunk=chunk, emit_state=emit_state),
        grid=(seqs, steps),
        in_specs=in_specs,
        out_specs=out_specs,
        out_shape=out_shape,
        scratch_shapes=scratch,
        compiler_params=pltpu.CompilerParams(
            dimension_semantics=("arbitrary", "arbitrary"), vmem_limit_bytes=VMEM_LIMIT),
        name="mixer",
    )(h, lw["norm_mix"], lw["w_main"], lw["w_if"], lw["b_if"], lw["conv_w"], lw["mh_gain"], lw["w_out"],
      lw["norm_ffn"], lw["w_route_t"], lw["b_route_t"], tri, *state)
    return outs[0], outs[1].reshape(seqs * seq_len), tuple(outs[2:])


def _expert_kernel(elo_ref, ehi_ref, nv_ref, nused_ref, inv_hbm, hx_hbm, nffn_ref, nfin_ref,
                   wg_lo, wg_hi, wu_lo, wu_hi, wd_lo, wd_hi, out_hbm,
                   idx_smem, xbuf, obuf, sem_i, sem_g, sem_s, *, tile, final_norm):
    j = pl.program_id(0)
    n_used = nused_ref[0]
    gather = RowGather(inv_hbm, hx_hbm, idx_smem, xbuf, sem_i, sem_g, tile)

    def scatter_wait(s):
        @pl.when(s >= 0)
        def _():
            rows = obuf.at[s % 2, pl.ds(0, nv_ref[jnp.maximum(s, 0)])]
            pltpu.make_async_copy(rows, rows, sem_s.at[s % 2]).wait()

    def step(phase):
        i_slot, o_slot = phase % IDX_SLOTS, phase % 2
        slot = gather.rows_ready(j, phase)
        x = xbuf[slot].reshape(tile, ROW_W)
        gather.prefetch(j, phase)
        hrow = x[:, 0:D_MODEL]
        w_lo = x[:, D_MODEL + 1:D_MODEL + 2]
        w_hi = x[:, D_MODEL + 2:D_MODEL + 3]
        xn = _rms(hrow, nffn_ref[...]).astype(_BF16)
        hid_lo = jax.nn.silu(_dot(xn, wg_lo[0])) * _dot(xn, wu_lo[0]) * w_lo
        hid_hi = jax.nn.silu(_dot(xn, wg_hi[0])) * _dot(xn, wu_hi[0]) * w_hi
        y = hrow + _dot(hid_lo.astype(_BF16), wd_lo[0]) + _dot(hid_hi.astype(_BF16), wd_hi[0])
        if final_norm:
            y = _rms(y, nfin_ref[...])
        scatter_wait(j - 2)
        obuf[o_slot, :, 0, :] = y

        nv = nv_ref[j]
        full = nv // SUBLANES

        def send(i):
            pltpu.make_async_copy(obuf.at[o_slot, i], out_hbm.at[pl.ds(idx_smem[i_slot, i], 1)],
                                  sem_s.at[o_slot]).start()

        def send_group(g):
            for k in range(SUBLANES):
                send(g * SUBLANES + k)

        def send_row(i, carry):
            send(i)
            return carry
        for g in range(tile // SUBLANES):
            pl.when(g < full)(functools.partial(send_group, g))
        lax.fori_loop(full * SUBLANES, nv, send_row, 0)

    gather.first(j)

    @pl.when(j < n_used)
    def _():
        _by_phase(j, step)

    @pl.when(j == n_used)
    def _():
        gather.drain(j)
        scatter_wait(j - 2)
        scatter_wait(j - 1)


_PAIR_LO = np.array([0, 0, 0, 1, 1, 2], np.int32)
_PAIR_HI = np.array([1, 2, 3, 2, 3, 3], np.int32)


def _sort_plan(bucket, n_tokens, tile):
    n_tiles = -(-n_tokens // tile) + N_BUCKETS + 1
    ids = jnp.arange(N_BUCKETS, dtype=jnp.int32)
    order = jnp.argsort(bucket, stable=True).astype(jnp.int32)
    counts = jnp.sum((bucket[:, None] == ids[None, :]).astype(jnp.int32), axis=0)
    tiles = (counts + tile - 1) // tile
    tile_end = jnp.cumsum(tiles)
    tile_start = tile_end - tiles
    cnt_start = jnp.cumsum(counts) - counts
    n_used = tile_end[-1]
    j = jnp.arange(n_tiles + LOOKAHEAD_TILES, dtype=jnp.int32)
    jj = jnp.minimum(j, jnp.maximum(n_used - 1, 0))
    sel = (jj[:, None] >= tile_start[None, :]) & (jj[:, None] < tile_end[None, :])
    pick = lambda v: jnp.sum(jnp.where(sel, v[None, :], 0), axis=1)
    tb = pick(ids)
    first = pick(cnt_start) + (jj - pick(tile_start)) * tile
    nv = jnp.where(j < n_used, jnp.clip(pick(cnt_start + counts) - first, 0, tile), 0).astype(jnp.int32)
    k = jnp.arange(tile, dtype=jnp.int32)
    slot = jnp.clip(first[:, None] + k[None, :], 0, n_tokens - 1)
    spread = (j[:, None] * tile + k[None, :]) % n_tokens
    inv = jnp.where(k[None, :] < nv[:, None], order[slot], spread).reshape(-1).astype(jnp.int32)
    grp = tb // N_PAIRS
    pr = tb % N_PAIRS
    e_lo = (grp * EXP_PER_GROUP + jnp.asarray(_PAIR_LO)[pr]).astype(jnp.int32)
    e_hi = (grp * EXP_PER_GROUP + jnp.asarray(_PAIR_HI)[pr]).astype(jnp.int32)
    return e_lo, e_hi, nv, n_used.reshape(1).astype(jnp.int32), inv, n_tiles


def _experts(hx, bucket, lw, norm_final, final_norm):
    n_tokens = hx.shape[0]
    tile = min(EXPERT_ROWS, n_tokens)
    e_lo, e_hi, nv, n_used, inv, n_tiles = _sort_plan(bucket, n_tokens, tile)
    inv, idx_rows = _idx_tiles(inv, tile)
    any_spec = pl.BlockSpec(memory_space=pl.ANY)
    vec = pl.BlockSpec((1, D_MODEL), lambda j, lo, hi, nv, nu: (0, 0))
    w_in_lo = pl.BlockSpec((1, D_MODEL, D_EXPERT), lambda j, lo, hi, nv, nu: (lo[j], 0, 0))
    w_in_hi = pl.BlockSpec((1, D_MODEL, D_EXPERT), lambda j, lo, hi, nv, nu: (hi[j], 0, 0))
    w_dn_lo = pl.BlockSpec((1, D_EXPERT, D_MODEL), lambda j, lo, hi, nv, nu: (lo[j], 0, 0))
    w_dn_hi = pl.BlockSpec((1, D_EXPERT, D_MODEL), lambda j, lo, hi, nv, nu: (hi[j], 0, 0))
    return pl.pallas_call(
        functools.partial(_expert_kernel, tile=tile, final_norm=final_norm),
        grid_spec=pltpu.PrefetchScalarGridSpec(
            num_scalar_prefetch=4,
            grid=(n_tiles,),
            in_specs=[any_spec, any_spec, vec, vec, w_in_lo, w_in_hi, w_in_lo, w_in_hi, w_dn_lo, w_dn_hi],
            out_specs=any_spec,
            scratch_shapes=[
                pltpu.SMEM((IDX_SLOTS, idx_rows), jnp.int32),
                _row_buffer(tile, ROW_W),
                pltpu.VMEM((2, tile, 1, D_MODEL), _F32),
                pltpu.SemaphoreType.DMA((IDX_SLOTS,)),
                pltpu.SemaphoreType.DMA((2,)),
                pltpu.SemaphoreType.DMA((2,)),
            ],
        ),
        out_shape=jax.ShapeDtypeStruct((n_tokens, D_MODEL), _F32),
        compiler_params=pltpu.CompilerParams(
            dimension_semantics=("arbitrary",), vmem_limit_bytes=VMEM_LIMIT),
        name="experts",
    )(e_lo, e_hi, nv, n_used, inv, hx, lw["norm_ffn"], norm_final,
      lw["w_gate"], lw["w_gate"], lw["w_up"], lw["w_up"], lw["w_down"], lw["w_down"])


def _layer_weights(l, norm_mix, w_in, b_if, conv_w, mh_gain, w_out, norm_ffn,
                   w_group, b_group, w_router, b_router, w_gate, w_up, w_down):
    w = w_in[l].astype(_BF16)
    o_if = 2 * QK_W + V_W
    o_rest = o_if + 2 * M_HEADS
    w_main = jnp.concatenate([w[:, :o_if], w[:, o_rest:]], axis=1)
    w_if = jnp.pad(w[:, o_if:o_rest], ((0, 0), (0, LANES - 2 * M_HEADS)))
    pad_lanes = lambda a: jnp.pad(a, ((0, 0), (0, LANES - a.shape[1])))
    n_logits = N_GROUPS + N_GROUPS * EXP_PER_GROUP
    w_route = jnp.pad(jnp.concatenate([w_group[l], w_router[l]], axis=1).T, ((0, ROUTE_ROWS - n_logits), (0, 0)))
    w_route_hi = w_route.astype(_BF16)
    w_route_lo = (w_route - w_route_hi.astype(_F32)).astype(_BF16)
    b_route = jnp.pad(jnp.concatenate([b_group[l], b_router[l]]), (0, ROUTE_ROWS - n_logits))
    return {
        "norm_mix": norm_mix[l][None], "w_main": w_main, "w_if": w_if,
        "b_if": pad_lanes(b_if[l][None]), "conv_w": conv_w[l], "mh_gain": mh_gain[l][None],
        "w_out": w_out[l].astype(_BF16), "norm_ffn": norm_ffn[l][None],
        "w_route_t": jnp.concatenate([w_route_hi, w_route_lo], axis=0), "b_route_t": b_route[:, None],
        "w_gate": w_gate[l].astype(_BF16), "w_up": w_up[l].astype(_BF16), "w_down": w_down[l].astype(_BF16),
    }


def kernel(x, meta_tokens, norm_mix, w_in, b_if, conv_w, mh_gain, w_out, norm_ffn, w_group, b_group,
           w_router, b_router, w_gate, w_up, w_down, norm_final):
    batch, seq, d = x.shape
    depth = w_in.shape[0]
    assert d == D_MODEL and seq % MIX_ROWS == 0 and meta_tokens.shape == (N_META, D_MODEL)
    layers = [_layer_weights(l, norm_mix, w_in, b_if, conv_w, mh_gain, w_out, norm_ffn,
                             w_group, b_group, w_router, b_router, w_gate, w_up, w_down)
              for l in range(depth)]
    zero_state = (jnp.zeros((M_HEADS, M_DQK, M_DV), _F32), jnp.zeros((SUBLANES, M_DQK), _F32),
                  jnp.zeros((SUBLANES, LANES), _F32), jnp.zeros((SUBLANES, D_MODEL), _F32))

    hm = jnp.concatenate([jnp.zeros((REF_CHUNK - N_META, D_MODEL), _F32), meta_tokens.astype(_F32)], axis=0)
    nfin = norm_final[None]
    states = []
    for l in range(depth):
        hmx, info, st = _mixer(hm, 1, REF_CHUNK, layers[l], zero_state, emit_state=True)
        states.append(st)
        if l + 1 < depth:
            hm = _experts(hmx, info, layers[l], nfin, False)

    h = x.reshape(batch * seq, D_MODEL)
    for l in range(depth):
        hx, info, _ = _mixer(h, batch, seq, layers[l], states[l])
        h = _experts(hx, info, layers[l], nfin, l + 1 == depth)
    return h.reshape(batch, seq, D_MODEL)
```

```python
import functools

import jax
import jax.numpy as jnp
import numpy as np
from jax import lax
from jax.experimental import pallas as pl
from jax.experimental.pallas import tpu as pltpu

D_MODEL = 1024
N_META = 16
M_HEADS = 4
M_DQK = 128
M_DV = 256
REF_CHUNK = 64
N_GROUPS = 4
EXP_PER_GROUP = 4
N_PAIRS = 6
N_BUCKETS = N_GROUPS * N_PAIRS
D_EXPERT = 256
EPS = 1e-6
QK_W = M_HEADS * M_DQK
V_W = M_HEADS * M_DV

LANES = 128
SUBLANES = 8
ROW_W = D_MODEL + LANES
ROUTE_ROWS = 32
VMEM_LIMIT = 56 * 1024 * 1024

_OFF_Q = 0
_OFF_K = _OFF_Q + QK_W
_OFF_V = _OFF_K + QK_W
_OFF_GM = _OFF_V + V_W
_OFF_CX = _OFF_GM + V_W
_OFF_CB = _OFF_CX + D_MODEL
_OFF_CC = _OFF_CB + D_MODEL
_OFF_GC = _OFF_CC + D_MODEL
MAIN_W = _OFF_GC + D_MODEL

MIX_CHUNK = 256
MIX_ROWS = 512
EXPERT_ROWS = 256

_F32 = jnp.float32
_BF16 = jnp.bfloat16


def _rms(x, g):
    return x * lax.rsqrt(jnp.mean(x * x, axis=-1, keepdims=True) + EPS) * g


def _dot(a, b):
    return jnp.dot(a, b, preferred_element_type=_F32)


def _split2(x):
    hi = x.astype(_BF16)
    return hi, (x - hi.astype(_F32)).astype(_BF16)


MIN_IDX_TILE = 128


LOOKAHEAD_TILES = 2
IDX_SLOTS = 4


def _idx_tiles(idx, rows):
    idx_rows = max(rows, MIN_IDX_TILE)
    if idx_rows == rows:
        return idx, idx_rows
    return jnp.pad(idx.reshape(-1, rows), ((0, 0), (0, idx_rows - rows))).reshape(-1), idx_rows


def _row_buffer(rows, width):
    return pltpu.VMEM((2, rows // SUBLANES, SUBLANES, width), _F32)


PHASES = IDX_SLOTS


def _by_phase(step, fn):
    for phase in range(PHASES):
        pl.when(step % PHASES == phase)(functools.partial(fn, phase))


class RowGather:
    def __init__(self, idx_hbm, src_hbm, idx_smem, buf, sem_i, sem_g, rows):
        self.idx_hbm, self.src_hbm, self.idx_smem, self.buf = idx_hbm, src_hbm, idx_smem, buf
        self.sem_i, self.sem_g, self.rows = sem_i, sem_g, rows
        self.idx_rows = idx_smem.shape[1]

    def _idx_copy(self, s, slot):
        return pltpu.make_async_copy(self.idx_hbm.at[pl.ds(s * self.idx_rows, self.idx_rows)],
                                     self.idx_smem.at[slot], self.sem_i.at[slot])

    def _issue(self, phase):
        i_slot, b_slot = phase % IDX_SLOTS, phase % 2
        for i in range(self.rows):
            pltpu.make_async_copy(self.src_hbm.at[self.idx_smem[i_slot, i]],
                                  self.buf.at[b_slot, i // SUBLANES, pl.ds(i % SUBLANES, 1)],
                                  self.sem_g.at[b_slot]).start(priority=i % 2)

    def _wait_rows(self, slot):
        pltpu.make_async_copy(self.buf.at[slot], self.buf.at[slot], self.sem_g.at[slot]).wait()

    def first(self, step):
        @pl.when(step == 0)
        def _():
            self._idx_copy(step, 0).start()
            self._idx_copy(step, 0).wait()
            self._issue(0)
            self._idx_copy(step + 1, 1).start()

    def rows_ready(self, step, phase):
        self._idx_copy(step + 1, (phase + 1) % IDX_SLOTS).wait()
        self._wait_rows(phase % 2)
        return phase % 2

    def prefetch(self, step, phase):
        self._issue(phase + 1)
        self._idx_copy(step + 2, (phase + 2) % IDX_SLOTS).start()

    def drain(self, step):
        self._wait_rows(step % 2)
        self._idx_copy(step + 1, (step + 1) % IDX_SLOTS).wait()


def _route(lgt):
    g = [lgt[i:i + 1, :] for i in range(N_GROUPS)]
    gmax = jnp.maximum(jnp.maximum(g[0], g[1]), jnp.maximum(g[2], g[3]))
    gsum = sum(jnp.exp(gi - gmax) for gi in g)
    g_val = 1.0 / gsum
    g_idx = jnp.where(g[0] == gmax, 0, jnp.where(g[1] == gmax, 1, jnp.where(g[2] == gmax, 2, 3)))
    e = []
    for k in range(EXP_PER_GROUP):
        col = lambda i, k=k: lgt[N_GROUPS + EXP_PER_GROUP * i + k:N_GROUPS + EXP_PER_GROUP * i + k + 1, :]
        e.append(jnp.where(g_idx == 0, col(0), jnp.where(g_idx == 1, col(1), jnp.where(g_idx == 2, col(2), col(3)))))
    v1 = jnp.maximum(jnp.maximum(e[0], e[1]), jnp.maximum(e[2], e[3]))
    i1 = jnp.where(e[0] == v1, 0, jnp.where(e[1] == v1, 1, jnp.where(e[2] == v1, 2, 3)))
    neg = jnp.float32(-jnp.inf)
    r = [jnp.where(i1 == k, neg, e[k]) for k in range(EXP_PER_GROUP)]
    v2 = jnp.maximum(jnp.maximum(r[0], r[1]), jnp.maximum(r[2], r[3]))
    i2 = jnp.where((r[0] == v2) & (i1 != 0), 0,
                   jnp.where((r[1] == v2) & (i1 != 1), 1, jnp.where((r[2] == v2) & (i1 != 2), 2, 3)))
    ex = jnp.exp(v2 - v1)
    w1 = g_val / (1.0 + ex)
    w2 = g_val * ex / (1.0 + ex)
    lo = jnp.minimum(i1, i2)
    hi = jnp.maximum(i1, i2)
    w_lo = jnp.where(i1 < i2, w1, w2)
    w_hi = jnp.where(i1 < i2, w2, w1)
    pair = jnp.where(lo == 0, hi - 1, jnp.where(lo == 1, hi + 1, 5))
    bucket = (g_idx * N_PAIRS + pair).astype(_F32)
    row = lax.broadcasted_iota(jnp.int32, (SUBLANES, lgt.shape[1]), 0)
    return jnp.where(row == 0, bucket, jnp.where(row == 1, w_lo, jnp.where(row == 2, w_hi, 0.0)))


def _mixer_kernel(*refs, rows, chunk, emit_state):
    refs = list(refs)
    take = lambda n: [refs.pop(0) for _ in range(n)]
    (h_ref, nmix_ref, wmain_ref, wif_ref, bif_ref, convw_ref, gain_ref, wout_ref, nffn_ref, wrt_ref,
     brt_ref, tri_ref, c0_ref, n0_ref, m0_ref, u0_ref) = take(16)
    out_ref, bucket_ref = take(2)
    if emit_state:
        cN_ref, nN_ref, mN_ref, uN_ref = take(4)
    c_scr, n_scr, m_scr, ubuf, ybuf = take(5)
    t = pl.program_id(1)

    @pl.when(t == 0)
    def _():
        c_scr[...] = c0_ref[...]
        n_scr[...] = n0_ref[...]
        m_scr[...] = m0_ref[...]
        ubuf[0:SUBLANES, :] = u0_ref[...]

    h = h_ref[...]
    xn = _rms(h, nmix_ref[...]).astype(_BF16)

    def proj(off, width):
        return _dot(xn, wmain_ref[:, off:off + width])

    u = proj(_OFF_CC, D_MODEL) * proj(_OFF_CX, D_MODEL)
    ubuf[SUBLANES:SUBLANES + rows, :] = u
    conv = (convw_ref[0:1, :] * ubuf[SUBLANES - 2:SUBLANES - 2 + rows, :]
            + convw_ref[1:2, :] * ubuf[SUBLANES - 1:SUBLANES - 1 + rows, :]
            + convw_ref[2:3, :] * u)
    ubuf[0:SUBLANES, :] = ubuf[rows:rows + SUBLANES, :]
    ybuf[...] = jax.nn.sigmoid(proj(_OFF_GC, D_MODEL)) * (proj(_OFF_CB, D_MODEL) * conv)

    gates = _dot(xn, wif_ref[...]) + bif_ref[...]
    lane = lax.broadcasted_iota(jnp.int32, gates.shape, 1)
    is_f = (lane >= M_HEADS) & (lane < 2 * M_HEADS)
    logf = jnp.where(is_f, jax.nn.log_sigmoid(gates), 0.0)
    tri = tri_ref[...]
    f_hi = logf.astype(_BF16)
    f_r = logf - f_hi.astype(_F32)
    f_mid = f_r.astype(_BF16)
    f_lo = (f_r - f_mid.astype(_F32)).astype(_BF16)
    bcum = _dot(tri, f_hi) + _dot(tri, f_mid) + _dot(tri, f_lo)
    gates_t = gates.T
    bcum_t = bcum.T

    q_all = (proj(_OFF_Q, QK_W) * (M_DQK ** -0.5)).astype(_BF16)
    k_all = proj(_OFF_K, QK_W)
    v_all = proj(_OFF_V, V_W).astype(_BF16)
    gm_all = proj(_OFF_GM, V_W)

    ti = lax.broadcasted_iota(jnp.int32, (chunk, chunk), 0)
    si = lax.broadcasted_iota(jnp.int32, (chunk, chunk), 1)
    causal = si <= ti

    for c in range(rows // chunk):
        r0 = c * chunk
        for hd in range(M_HEADS):
            q = q_all[r0:r0 + chunk, hd * M_DQK:(hd + 1) * M_DQK]
            kf = k_all[r0:r0 + chunk, hd * M_DQK:(hd + 1) * M_DQK]
            v = v_all[r0:r0 + chunk, hd * M_DV:(hd + 1) * M_DV]
            b_c = bcum[r0:r0 + chunk, M_HEADS + hd:M_HEADS + hd + 1]
            i_c = gates[r0:r0 + chunk, hd:hd + 1]
            b_r = bcum_t[M_HEADS + hd:M_HEADS + hd + 1, r0:r0 + chunk]
            i_r = gates_t[hd:hd + 1, r0:r0 + chunk]
            m_prev = m_scr[hd:hd + 1, 0:1]
            c_prev = c_scr[hd]
            n_prev = n_scr[hd:hd + 1, :]

            log_d = jnp.where(causal, b_c - b_r + i_r, -jnp.inf)
            inter = b_c + m_prev
            m_t = jnp.maximum(inter, jnp.max(log_d, axis=-1, keepdims=True))
            dmat = jnp.exp(log_d - m_t)
            s = lax.dot_general(q, kf.astype(_BF16), (((1,), (1,)), ((), ())),
                                preferred_element_type=_F32) * dmat
            e_inter = jnp.exp(inter - m_t)
            num = _dot(s.astype(_BF16), v) + e_inter * _dot(q, c_prev.astype(_BF16))
            den = (jnp.sum(s, axis=-1, keepdims=True)
                   + e_inter * jnp.sum(q.astype(_F32) * n_prev, axis=-1, keepdims=True))
            hc = num / jnp.maximum(jnp.abs(den), jnp.exp(-m_t))

            b_last = b_c[chunk - 1:chunk, :]
            log_w = b_last - b_c + i_c
            m_new = jnp.maximum(b_last + m_prev, jnp.max(log_w, axis=0, keepdims=True))
            kw = kf * jnp.exp(log_w - m_new)
            decay = jnp.exp(b_last + m_prev - m_new)
            c_scr[hd] = decay * c_prev + lax.dot_general(
                kw.astype(_BF16), v, (((0,), (0,)), ((), ())), preferred_element_type=_F32)
            n_scr[hd:hd + 1, :] = decay * n_prev + jnp.sum(kw, axis=0, keepdims=True)
            m_scr[hd:hd + 1, :] = jnp.broadcast_to(m_new, (1, LANES))

            hn = hc * lax.rsqrt(jnp.mean(hc * hc, axis=-1, keepdims=True) + EPS)
            cols = slice(hd * M_DV, (hd + 1) * M_DV)
            ybuf[r0:r0 + chunk, cols] += (jax.nn.sigmoid(gm_all[r0:r0 + chunk, cols])
                                          * (hn * gain_ref[:, cols]))

    h1 = h + _dot(ybuf[...].astype(_BF16), wout_ref[...])

    x_hi, x_lo = _split2(_rms(h1, nffn_ref[...]))
    nt = (((1,), (1,)), ((), ()))
    hi_all = lax.dot_general(wrt_ref[...], x_hi, nt, preferred_element_type=_F32)
    lo_x = lax.dot_general(wrt_ref[0:ROUTE_ROWS, :], x_lo, nt, preferred_element_type=_F32)
    lgt = hi_all[0:ROUTE_ROWS] + hi_all[ROUTE_ROWS:2 * ROUTE_ROWS] + lo_x + brt_ref[...]
    info_t = _route(lgt)
    bucket_ref[0] = info_t[0:1, :].astype(jnp.int32)
    out_ref[:, 0, 0:D_MODEL] = h1
    out_ref[:, 0, D_MODEL:ROW_W] = jnp.concatenate(
        [info_t, jnp.zeros((LANES - SUBLANES, rows), _F32)], axis=0).T

    if emit_state:
        @pl.when(t == pl.num_programs(1) - 1)
        def _():
            cN_ref[...] = c_scr[...]
            nN_ref[...] = n_scr[...]
            mN_ref[...] = m_scr[...]
            uN_ref[...] = ubuf[0:SUBLANES, :]


def _mixer(h, seqs, seq_len, lw, state, emit_state=False):
    rows = min(MIX_ROWS, seq_len)
    chunk = min(MIX_CHUNK, rows)
    steps = seq_len // rows
    r = np.arange(rows)
    tri = jnp.asarray((r[:, None] // chunk == r[None, :] // chunk) & (r[None, :] <= r[:, None]), _BF16)
    const = lambda shape: pl.BlockSpec(shape, lambda b, t: (0,) * len(shape), pipeline_mode=pl.Buffered(1))
    state_shapes = [(M_HEADS, M_DQK, M_DV), (SUBLANES, M_DQK), (SUBLANES, LANES), (SUBLANES, D_MODEL)]
    in_specs = [
        pl.BlockSpec((rows, D_MODEL), lambda b, t: (b * steps + t, 0)),
        const((1, D_MODEL)), const((D_MODEL, MAIN_W)), const((D_MODEL, LANES)), const((1, LANES)),
        const((3, D_MODEL)), const((1, V_W)), const((D_MODEL, D_MODEL)), const((1, D_MODEL)),
        const((2 * ROUTE_ROWS, D_MODEL)), const((ROUTE_ROWS, 1)), const((rows, rows)),
    ] + [const(s) for s in state_shapes]
    out_specs = [pl.BlockSpec((rows, 1, ROW_W), lambda b, t: (b * steps + t, 0, 0)),
                 pl.BlockSpec((1, 1, rows), lambda b, t: (b * steps + t, 0, 0))]
    out_shape = [jax.ShapeDtypeStruct((seqs * seq_len, 1, ROW_W), _F32),
                 jax.ShapeDtypeStruct((seqs * steps, 1, rows), jnp.int32)]
    if emit_state:
        out_specs += [pl.BlockSpec(s, lambda b, t, n=len(s): (0,) * n) for s in state_shapes]
        out_shape += [jax.ShapeDtypeStruct(s, _F32) for s in state_shapes]
    scratch = [
        pltpu.VMEM((M_HEADS, M_DQK, M_DV), _F32),
        pltpu.VMEM((SUBLANES, M_DQK), _F32),
        pltpu.VMEM((SUBLANES, LANES), _F32),
        pltpu.VMEM((rows + SUBLANES, D_MODEL), _F32),
        pltpu.VMEM((rows, D_MODEL), _F32),
    ]
    outs = pl.pallas_call(
        functools.partial(_mixer_kernel, rows=rows, chunk=chunk, emit_state=emit_state),
        grid=(seqs, steps),
        in_specs=in_specs,
        out_specs=out_specs,
        out_shape=out_shape,
        scratch_shapes=scratch,
        compiler_params=pltpu.CompilerParams(
            dimension_semantics=("arbitrary", "arbitrary"), vmem_limit_bytes=VMEM_LIMIT),
        name="mixer",
    )(h, lw["norm_mix"], lw["w_main"], lw["w_if"], lw["b_if"], lw["conv_w"], lw["mh_gain"], lw["w_out"],
      lw["norm_ffn"], lw["w_route_t"], lw["b_route_t"], tri, *state)
    return outs[0], outs[1].reshape(seqs * seq_len), tuple(outs[2:])


def _expert_kernel(elo_ref, ehi_ref, nv_ref, nused_ref, inv_hbm, hx_hbm, nffn_ref, nfin_ref,
                   wg_lo, wg_hi, wu_lo, wu_hi, wd_lo, wd_hi, out_hbm,
                   idx_smem, xbuf, obuf, sem_i, sem_g, sem_s, *, tile, final_norm):
    j = pl.program_id(0)
    n_used = nused_ref[0]
    gather = RowGather(inv_hbm, hx_hbm, idx_smem, xbuf, sem_i, sem_g, tile)

    def scatter_wait(s):
        @pl.when(s >= 0)
        def _():
            rows = obuf.at[s % 2, pl.ds(0, nv_ref[jnp.maximum(s, 0)])]
            pltpu.make_async_copy(rows, rows, sem_s.at[s % 2]).wait()

    def step(phase):
        i_slot, o_slot = phase % IDX_SLOTS, phase % 2
        slot = gather.rows_ready(j, phase)
        x = xbuf[slot].reshape(tile, ROW_W)
        gather.prefetch(j, phase)
        hrow = x[:, 0:D_MODEL]
        w_lo = x[:, D_MODEL + 1:D_MODEL + 2]
        w_hi = x[:, D_MODEL + 2:D_MODEL + 3]
        xn = _rms(hrow, nffn_ref[...]).astype(_BF16)
        hid_lo = jax.nn.silu(_dot(xn, wg_lo[0])) * _dot(xn, wu_lo[0]) * w_lo
        hid_hi = jax.nn.silu(_dot(xn, wg_hi[0])) * _dot(xn, wu_hi[0]) * w_hi
        y = hrow + _dot(hid_lo.astype(_BF16), wd_lo[0]) + _dot(hid_hi.astype(_BF16), wd_hi[0])
        if final_norm:
            y = _rms(y, nfin_ref[...])
        scatter_wait(j - 2)
        obuf[o_slot, :, 0, :] = y

        nv = nv_ref[j]
        full = nv // SUBLANES

        def send(i, priority):
            pltpu.make_async_copy(obuf.at[o_slot, i], out_hbm.at[pl.ds(idx_smem[i_slot, i], 1)],
                                  sem_s.at[o_slot]).start(priority=priority)

        def send_group(g):
            for k in range(SUBLANES):
                send(g * SUBLANES + k, k % 2)

        def send_row(i, carry):
            send(i, 0)
            return carry
        for g in range(tile // SUBLANES):
            pl.when(g < full)(functools.partial(send_group, g))
        lax.fori_loop(full * SUBLANES, nv, send_row, 0)

    gather.first(j)

    @pl.when(j < n_used)
    def _():
        _by_phase(j, step)

    @pl.when(j == n_used)
    def _():
        gather.drain(j)
        scatter_wait(j - 2)
        scatter_wait(j - 1)


_PAIR_LO = np.array([0, 0, 0, 1, 1, 2], np.int32)
_PAIR_HI = np.array([1, 2, 3, 2, 3, 3], np.int32)


def _sort_plan(bucket, n_tokens, tile):
    n_tiles = -(-n_tokens // tile) + N_BUCKETS + 1
    ids = jnp.arange(N_BUCKETS, dtype=jnp.int32)
    order = jnp.argsort(bucket, stable=True).astype(jnp.int32)
    counts = jnp.sum((bucket[:, None] == ids[None, :]).astype(jnp.int32), axis=0)
    tiles = (counts + tile - 1) // tile
    tile_end = jnp.cumsum(tiles)
    tile_start = tile_end - tiles
    cnt_start = jnp.cumsum(counts) - counts
    n_used = tile_end[-1]
    j = jnp.arange(n_tiles + LOOKAHEAD_TILES, dtype=jnp.int32)
    jj = jnp.minimum(j, jnp.maximum(n_used - 1, 0))
    sel = (jj[:, None] >= tile_start[None, :]) & (jj[:, None] < tile_end[None, :])
    pick = lambda v: jnp.sum(jnp.where(sel, v[None, :], 0), axis=1)
    tb = pick(ids)
    first = pick(cnt_start) + (jj - pick(tile_start)) * tile
    nv = jnp.where(j < n_used, jnp.clip(pick(cnt_start + counts) - first, 0, tile), 0).astype(jnp.int32)
    k = jnp.arange(tile, dtype=jnp.int32)
    slot = jnp.clip(first[:, None] + k[None, :], 0, n_tokens - 1)
    spread = (j[:, None] * tile + k[None, :]) % n_tokens
    inv = jnp.where(k[None, :] < nv[:, None], order[slot], spread).reshape(-1).astype(jnp.int32)
    grp = tb // N_PAIRS
    pr = tb % N_PAIRS
    e_lo = (grp * EXP_PER_GROUP + jnp.asarray(_PAIR_LO)[pr]).astype(jnp.int32)
    e_hi = (grp * EXP_PER_GROUP + jnp.asarray(_PAIR_HI)[pr]).astype(jnp.int32)
    return e_lo, e_hi, nv, n_used.reshape(1).astype(jnp.int32), inv, n_tiles


def _experts(hx, bucket, lw, norm_final, final_norm):
    n_tokens = hx.shape[0]
    tile = min(EXPERT_ROWS, n_tokens)
    e_lo, e_hi, nv, n_used, inv, n_tiles = _sort_plan(bucket, n_tokens, tile)
    inv, idx_rows = _idx_tiles(inv, tile)
    any_spec = pl.BlockSpec(memory_space=pl.ANY)
    vec = pl.BlockSpec((1, D_MODEL), lambda j, lo, hi, nv, nu: (0, 0))
    w_in_lo = pl.BlockSpec((1, D_MODEL, D_EXPERT), lambda j, lo, hi, nv, nu: (lo[j], 0, 0))
    w_in_hi = pl.BlockSpec((1, D_MODEL, D_EXPERT), lambda j, lo, hi, nv, nu: (hi[j], 0, 0))
    w_dn_lo = pl.BlockSpec((1, D_EXPERT, D_MODEL), lambda j, lo, hi, nv, nu: (lo[j], 0, 0))
    w_dn_hi = pl.BlockSpec((1, D_EXPERT, D_MODEL), lambda j, lo, hi, nv, nu: (hi[j], 0, 0))
    return pl.pallas_call(
        functools.partial(_expert_kernel, tile=tile, final_norm=final_norm),
        grid_spec=pltpu.PrefetchScalarGridSpec(
            num_scalar_prefetch=4,
            grid=(n_tiles,),
            in_specs=[any_spec, any_spec, vec, vec, w_in_lo, w_in_hi, w_in_lo, w_in_hi, w_dn_lo, w_dn_hi],
            out_specs=any_spec,
            scratch_shapes=[
                pltpu.SMEM((IDX_SLOTS, idx_rows), jnp.int32),
                _row_buffer(tile, ROW_W),
                pltpu.VMEM((2, tile, 1, D_MODEL), _F32),
                pltpu.SemaphoreType.DMA((IDX_SLOTS,)),
                pltpu.SemaphoreType.DMA((2,)),
                pltpu.SemaphoreType.DMA((2,)),
            ],
        ),
        out_shape=jax.ShapeDtypeStruct((n_tokens, D_MODEL), _F32),
        compiler_params=pltpu.CompilerParams(
            dimension_semantics=("arbitrary",), vmem_limit_bytes=VMEM_LIMIT),
        name="experts",
    )(e_lo, e_hi, nv, n_used, inv, hx, lw["norm_ffn"], norm_final,
      lw["w_gate"], lw["w_gate"], lw["w_up"], lw["w_up"], lw["w_down"], lw["w_down"])


def _layer_weights(l, norm_mix, w_in, b_if, conv_w, mh_gain, w_out, norm_ffn,
                   w_group, b_group, w_router, b_router, w_gate, w_up, w_down):
    w = w_in[l].astype(_BF16)
    o_if = 2 * QK_W + V_W
    o_rest = o_if + 2 * M_HEADS
    w_main = jnp.concatenate([w[:, :o_if], w[:, o_rest:]], axis=1)
    w_if = jnp.pad(w[:, o_if:o_rest], ((0, 0), (0, LANES - 2 * M_HEADS)))
    pad_lanes = lambda a: jnp.pad(a, ((0, 0), (0, LANES - a.shape[1])))
    n_logits = N_GROUPS + N_GROUPS * EXP_PER_GROUP
    w_route = jnp.pad(jnp.concatenate([w_group[l], w_router[l]], axis=1).T, ((0, ROUTE_ROWS - n_logits), (0, 0)))
    w_route_hi = w_route.astype(_BF16)
    w_route_lo = (w_route - w_route_hi.astype(_F32)).astype(_BF16)
    b_route = jnp.pad(jnp.concatenate([b_group[l], b_router[l]]), (0, ROUTE_ROWS - n_logits))
    return {
        "norm_mix": norm_mix[l][None], "w_main": w_main, "w_if": w_if,
        "b_if": pad_lanes(b_if[l][None]), "conv_w": conv_w[l], "mh_gain": mh_gain[l][None],
        "w_out": w_out[l].astype(_BF16), "norm_ffn": norm_ffn[l][None],
        "w_route_t": jnp.concatenate([w_route_hi, w_route_lo], axis=0), "b_route_t": b_route[:, None],
        "w_gate": w_gate[l].astype(_BF16), "w_up": w_up[l].astype(_BF16), "w_down": w_down[l].astype(_BF16),
    }


def kernel(x, meta_tokens, norm_mix, w_in, b_if, conv_w, mh_gain, w_out, norm_ffn, w_group, b_group,
           w_router, b_router, w_gate, w_up, w_down, norm_final):
    batch, seq, d = x.shape
    depth = w_in.shape[0]
    assert d == D_MODEL and seq % MIX_ROWS == 0 and meta_tokens.shape == (N_META, D_MODEL)
    layers = [_layer_weights(l, norm_mix, w_in, b_if, conv_w, mh_gain, w_out, norm_ffn,
                             w_group, b_group, w_router, b_router, w_gate, w_up, w_down)
              for l in range(depth)]
    zero_state = (jnp.zeros((M_HEADS, M_DQK, M_DV), _F32), jnp.zeros((SUBLANES, M_DQK), _F32),
                  jnp.zeros((SUBLANES, LANES), _F32), jnp.zeros((SUBLANES, D_MODEL), _F32))

    hm = jnp.concatenate([jnp.zeros((REF_CHUNK - N_META, D_MODEL), _F32), meta_tokens.astype(_F32)], axis=0)
    nfin = norm_final[None]
    states = []
    for l in range(depth):
        hmx, info, st = _mixer(hm, 1, REF_CHUNK, layers[l], zero_state, emit_state=True)
        states.append(st)
        if l + 1 < depth:
            hm = _experts(hmx, info, layers[l], nfin, False)

    h = x.reshape(batch * seq, D_MODEL)
    for l in range(depth):
        hx, info, _ = _mixer(h, batch, seq, layers[l], states[l])
        h = _experts(hx, info, layers[l], nfin, l + 1 == depth)
    return h.reshape(batch, seq, D_MODEL)
```

```python
import functools

import jax
import jax.numpy as jnp
import numpy as np
from jax import lax
from jax.experimental import pallas as pl
from jax.experimental.pallas import tpu as pltpu

D_MODEL = 1024
N_META = 16
M_HEADS = 4
M_DQK = 128
M_DV = 256
REF_CHUNK = 64
N_GROUPS = 4
EXP_PER_GROUP = 4
N_PAIRS = 6
N_BUCKETS = N_GROUPS * N_PAIRS
D_EXPERT = 256
EPS = 1e-6
QK_W = M_HEADS * M_DQK
V_W = M_HEADS * M_DV

LANES = 128
SUBLANES = 8
ROW_W = D_MODEL + LANES
ROUTE_ROWS = 32
VMEM_LIMIT = 56 * 1024 * 1024

_OFF_Q = 0
_OFF_K = _OFF_Q + QK_W
_OFF_V = _OFF_K + QK_W
_OFF_GM = _OFF_V + V_W
_OFF_CX = _OFF_GM + V_W
_OFF_CB = _OFF_CX + D_MODEL
_OFF_CC = _OFF_CB + D_MODEL
_OFF_GC = _OFF_CC + D_MODEL
MAIN_W = _OFF_GC + D_MODEL

MIX_CHUNK = 256
MIX_ROWS = 512
EXPERT_ROWS = 256

_F32 = jnp.float32
_BF16 = jnp.bfloat16


def _rms(x, g):
    return x * lax.rsqrt(jnp.mean(x * x, axis=-1, keepdims=True) + EPS) * g


def _dot(a, b):
    return jnp.dot(a, b, preferred_element_type=_F32)


def _split2(x):
    hi = x.astype(_BF16)
    return hi, (x - hi.astype(_F32)).astype(_BF16)


MIN_IDX_TILE = 128


LOOKAHEAD_TILES = 2
IDX_SLOTS = 4


def _idx_tiles(idx, rows):
    idx_rows = max(rows, MIN_IDX_TILE)
    if idx_rows == rows:
        return idx, idx_rows
    return jnp.pad(idx.reshape(-1, rows), ((0, 0), (0, idx_rows - rows))).reshape(-1), idx_rows


def _row_buffer(rows, width):
    return pltpu.VMEM((2, rows // SUBLANES, SUBLANES, width), _F32)


PHASES = IDX_SLOTS


def _by_phase(step, fn):
    for phase in range(PHASES):
        pl.when(step % PHASES == phase)(functools.partial(fn, phase))


class RowGather:
    def __init__(self, idx_hbm, src_hbm, idx_smem, buf, sem_i, sem_g, rows):
        self.idx_hbm, self.src_hbm, self.idx_smem, self.buf = idx_hbm, src_hbm, idx_smem, buf
        self.sem_i, self.sem_g, self.rows = sem_i, sem_g, rows
        self.idx_rows = idx_smem.shape[1]

    def _idx_copy(self, s, slot):
        return pltpu.make_async_copy(self.idx_hbm.at[pl.ds(s * self.idx_rows, self.idx_rows)],
                                     self.idx_smem.at[slot], self.sem_i.at[slot])

    def _issue(self, phase):
        i_slot, b_slot = phase % IDX_SLOTS, phase % 2
        for i in range(self.rows):
            pltpu.make_async_copy(self.src_hbm.at[self.idx_smem[i_slot, i]],
                                  self.buf.at[b_slot, i // SUBLANES, pl.ds(i % SUBLANES, 1)],
                                  self.sem_g.at[b_slot]).start(priority=0)

    def _wait_rows(self, slot):
        pltpu.make_async_copy(self.buf.at[slot], self.buf.at[slot], self.sem_g.at[slot]).wait()

    def first(self, step):
        @pl.when(step == 0)
        def _():
            self._idx_copy(step, 0).start()
            self._idx_copy(step, 0).wait()
            self._issue(0)
            self._idx_copy(step + 1, 1).start()

    def rows_ready(self, step, phase):
        self._idx_copy(step + 1, (phase + 1) % IDX_SLOTS).wait()
        self._wait_rows(phase % 2)
        return phase % 2

    def prefetch(self, step, phase):
        self._issue(phase + 1)
        self._idx_copy(step + 2, (phase + 2) % IDX_SLOTS).start()

    def drain(self, step):
        self._wait_rows(step % 2)
        self._idx_copy(step + 1, (step + 1) % IDX_SLOTS).wait()


def _route(lgt):
    g = [lgt[i:i + 1, :] for i in range(N_GROUPS)]
    gmax = jnp.maximum(jnp.maximum(g[0], g[1]), jnp.maximum(g[2], g[3]))
    gsum = sum(jnp.exp(gi - gmax) for gi in g)
    g_val = 1.0 / gsum
    g_idx = jnp.where(g[0] == gmax, 0, jnp.where(g[1] == gmax, 1, jnp.where(g[2] == gmax, 2, 3)))
    e = []
    for k in range(EXP_PER_GROUP):
        col = lambda i, k=k: lgt[N_GROUPS + EXP_PER_GROUP * i + k:N_GROUPS + EXP_PER_GROUP * i + k + 1, :]
        e.append(jnp.where(g_idx == 0, col(0), jnp.where(g_idx == 1, col(1), jnp.where(g_idx == 2, col(2), col(3)))))
    v1 = jnp.maximum(jnp.maximum(e[0], e[1]), jnp.maximum(e[2], e[3]))
    i1 = jnp.where(e[0] == v1, 0, jnp.where(e[1] == v1, 1, jnp.where(e[2] == v1, 2, 3)))
    neg = jnp.float32(-jnp.inf)
    r = [jnp.where(i1 == k, neg, e[k]) for k in range(EXP_PER_GROUP)]
    v2 = jnp.maximum(jnp.maximum(r[0], r[1]), jnp.maximum(r[2], r[3]))
    i2 = jnp.where((r[0] == v2) & (i1 != 0), 0,
                   jnp.where((r[1] == v2) & (i1 != 1), 1, jnp.where((r[2] == v2) & (i1 != 2), 2, 3)))
    ex = jnp.exp(v2 - v1)
    w1 = g_val / (1.0 + ex)
    w2 = g_val * ex / (1.0 + ex)
    lo = jnp.minimum(i1, i2)
    hi = jnp.maximum(i1, i2)
    w_lo = jnp.where(i1 < i2, w1, w2)
    w_hi = jnp.where(i1 < i2, w2, w1)
    pair = jnp.where(lo == 0, hi - 1, jnp.where(lo == 1, hi + 1, 5))
    bucket = (g_idx * N_PAIRS + pair).astype(_F32)
    row = lax.broadcasted_iota(jnp.int32, (SUBLANES, lgt.shape[1]), 0)
    return jnp.where(row == 0, bucket, jnp.where(row == 1, w_lo, jnp.where(row == 2, w_hi, 0.0)))


def _mixer_kernel(*refs, rows, chunk, emit_state):
    refs = list(refs)
    take = lambda n: [refs.pop(0) for _ in range(n)]
    (h_ref, nmix_ref, wmain_ref, wif_ref, bif_ref, convw_ref, gain_ref, wout_ref, nffn_ref, wrt_ref,
     brt_ref, tri_ref, c0_ref, n0_ref, m0_ref, u0_ref) = take(16)
    out_ref, bucket_ref = take(2)
    if emit_state:
        cN_ref, nN_ref, mN_ref, uN_ref = take(4)
    c_scr, n_scr, m_scr, ubuf, ybuf = take(5)
    t = pl.program_id(1)

    @pl.when(t == 0)
    def _():
        c_scr[...] = c0_ref[...]
        n_scr[...] = n0_ref[...]
        m_scr[...] = m0_ref[...]
        ubuf[0:SUBLANES, :] = u0_ref[...]

    h = h_ref[...]
    xn = _rms(h, nmix_ref[...]).astype(_BF16)

    def proj(off, width):
        return _dot(xn, wmain_ref[:, off:off + width])

    u = proj(_OFF_CC, D_MODEL) * proj(_OFF_CX, D_MODEL)
    ubuf[SUBLANES:SUBLANES + rows, :] = u
    conv = (convw_ref[0:1, :] * ubuf[SUBLANES - 2:SUBLANES - 2 + rows, :]
            + convw_ref[1:2, :] * ubuf[SUBLANES - 1:SUBLANES - 1 + rows, :]
            + convw_ref[2:3, :] * u)
    ubuf[0:SUBLANES, :] = ubuf[rows:rows + SUBLANES, :]
    ybuf[...] = jax.nn.sigmoid(proj(_OFF_GC, D_MODEL)) * (proj(_OFF_CB, D_MODEL) * conv)

    gates = _dot(xn, wif_ref[...]) + bif_ref[...]
    lane = lax.broadcasted_iota(jnp.int32, gates.shape, 1)
    is_f = (lane >= M_HEADS) & (lane < 2 * M_HEADS)
    logf = jnp.where(is_f, jax.nn.log_sigmoid(gates), 0.0)
    tri = tri_ref[...]
    f_hi = logf.astype(_BF16)
    f_r = logf - f_hi.astype(_F32)
    f_mid = f_r.astype(_BF16)
    f_lo = (f_r - f_mid.astype(_F32)).astype(_BF16)
    bcum = _dot(tri, f_hi) + _dot(tri, f_mid) + _dot(tri, f_lo)
    gates_t = gates.T
    bcum_t = bcum.T

    q_all = (proj(_OFF_Q, QK_W) * (M_DQK ** -0.5)).astype(_BF16)
    k_all = proj(_OFF_K, QK_W)
    v_all = proj(_OFF_V, V_W).astype(_BF16)
    gm_all = proj(_OFF_GM, V_W)

    ti = lax.broadcasted_iota(jnp.int32, (chunk, chunk), 0)
    si = lax.broadcasted_iota(jnp.int32, (chunk, chunk), 1)
    causal = si <= ti

    for c in range(rows // chunk):
        r0 = c * chunk
        for hd in range(M_HEADS):
            q = q_all[r0:r0 + chunk, hd * M_DQK:(hd + 1) * M_DQK]
            kf = k_all[r0:r0 + chunk, hd * M_DQK:(hd + 1) * M_DQK]
            v = v_all[r0:r0 + chunk, hd * M_DV:(hd + 1) * M_DV]
            b_c = bcum[r0:r0 + chunk, M_HEADS + hd:M_HEADS + hd + 1]
            i_c = gates[r0:r0 + chunk, hd:hd + 1]
            b_r = bcum_t[M_HEADS + hd:M_HEADS + hd + 1, r0:r0 + chunk]
            i_r = gates_t[hd:hd + 1, r0:r0 + chunk]
            m_prev = m_scr[hd:hd + 1, 0:1]
            c_prev = c_scr[hd]
            n_prev = n_scr[hd:hd + 1, :]

            log_d = jnp.where(causal, b_c - b_r + i_r, -jnp.inf)
            inter = b_c + m_prev
            m_t = jnp.maximum(inter, jnp.max(log_d, axis=-1, keepdims=True))
            dmat = jnp.exp(log_d - m_t)
            s = lax.dot_general(q, kf.astype(_BF16), (((1,), (1,)), ((), ())),
                                preferred_element_type=_F32) * dmat
            e_inter = jnp.exp(inter - m_t)
            num = _dot(s.astype(_BF16), v) + e_inter * _dot(q, c_prev.astype(_BF16))
            den = (jnp.sum(s, axis=-1, keepdims=True)
                   + e_inter * jnp.sum(q.astype(_F32) * n_prev, axis=-1, keepdims=True))
            hc = num / jnp.maximum(jnp.abs(den), jnp.exp(-m_t))

            b_last = b_c[chunk - 1:chunk, :]
            log_w = b_last - b_c + i_c
            m_new = jnp.maximum(b_last + m_prev, jnp.max(log_w, axis=0, keepdims=True))
            kw = kf * jnp.exp(log_w - m_new)
            decay = jnp.exp(b_last + m_prev - m_new)
            c_scr[hd] = decay * c_prev + lax.dot_general(
                kw.astype(_BF16), v, (((0,), (0,)), ((), ())), preferred_element_type=_F32)
            n_scr[hd:hd + 1, :] = decay * n_prev + jnp.sum(kw, axis=0, keepdims=True)
            m_scr[hd:hd + 1, :] = jnp.broadcast_to(m_new, (1, LANES))

            hn = hc * lax.rsqrt(jnp.mean(hc * hc, axis=-1, keepdims=True) + EPS)
            cols = slice(hd * M_DV, (hd + 1) * M_DV)
            ybuf[r0:r0 + chunk, cols] += (jax.nn.sigmoid(gm_all[r0:r0 + chunk, cols])
                                          * (hn * gain_ref[:, cols]))

    h1 = h + _dot(ybuf[...].astype(_BF16), wout_ref[...])

    x_hi, x_lo = _split2(_rms(h1, nffn_ref[...]))
    nt = (((1,), (1,)), ((), ()))
    hi_all = lax.dot_general(wrt_ref[...], x_hi, nt, preferred_element_type=_F32)
    lo_x = lax.dot_general(wrt_ref[0:ROUTE_ROWS, :], x_lo, nt, preferred_element_type=_F32)
    lgt = hi_all[0:ROUTE_ROWS] + hi_all[ROUTE_ROWS:2 * ROUTE_ROWS] + lo_x + brt_ref[...]
    info_t = _route(lgt)
    bucket_ref[0] = info_t[0:1, :].astype(jnp.int32)
    out_ref[:, 0, 0:D_MODEL] = h1
    out_ref[:, 0, D_MODEL:ROW_W] = jnp.concatenate(
        [info_t, jnp.zeros((LANES - SUBLANES, rows), _F32)], axis=0).T

    if emit_state:
        @pl.when(t == pl.num_programs(1) - 1)
        def _():
            cN_ref[...] = c_scr[...]
            nN_ref[...] = n_scr[...]
            mN_ref[...] = m_scr[...]
            uN_ref[...] = ubuf[0:SUBLANES, :]


def _mixer(h, seqs, seq_len, lw, state, emit_state=False):
    rows = min(MIX_ROWS, seq_len)
    chunk = min(MIX_CHUNK, rows)
    steps = seq_len // rows
    r = np.arange(rows)
    tri = jnp.asarray((r[:, None] // chunk == r[None, :] // chunk) & (r[None, :] <= r[:, None]), _BF16)
    const = lambda shape: pl.BlockSpec(shape, lambda b, t: (0,) * len(shape), pipeline_mode=pl.Buffered(1))
    state_shapes = [(M_HEADS, M_DQK, M_DV), (SUBLANES, M_DQK), (SUBLANES, LANES), (SUBLANES, D_MODEL)]
    in_specs = [
        pl.BlockSpec((rows, D_MODEL), lambda b, t: (b * steps + t, 0)),
        const((1, D_MODEL)), const((D_MODEL, MAIN_W)), const((D_MODEL, LANES)), const((1, LANES)),
        const((3, D_MODEL)), const((1, V_W)), const((D_MODEL, D_MODEL)), const((1, D_MODEL)),
        const((2 * ROUTE_ROWS, D_MODEL)), const((ROUTE_ROWS, 1)), const((rows, rows)),
    ] + [const(s) for s in state_shapes]
    out_specs = [pl.BlockSpec((rows, 1, ROW_W), lambda b, t: (b * steps + t, 0, 0)),
                 pl.BlockSpec((1, 1, rows), lambda b, t: (b * steps + t, 0, 0))]
    out_shape = [jax.ShapeDtypeStruct((seqs * seq_len, 1, ROW_W), _F32),
                 jax.ShapeDtypeStruct((seqs * steps, 1, rows), jnp.int32)]
    if emit_state:
        out_specs += [pl.BlockSpec(s, lambda b, t, n=len(s): (0,) * n) for s in state_shapes]
        out_shape += [jax.ShapeDtypeStruct(s, _F32) for s in state_shapes]
    scratch = [
        pltpu.VMEM((M_HEADS, M_DQK, M_DV), _F32),
        pltpu.VMEM((SUBLANES, M_DQK), _F32),
        pltpu.VMEM((SUBLANES, LANES), _F32),
        pltpu.VMEM((rows + SUBLANES, D_MODEL), _F32),
        pltpu.VMEM((rows, D_MODEL), _F32),
    ]
    outs = pl.pallas_call(
        functools.partial(_mixer_kernel, rows=rows, chunk=chunk, emit_state=emit_state),
        grid=(seqs, steps),
        in_specs=in_specs,
        out_specs=out_specs,
        out_shape=out_shape,
        scratch_shapes=scratch,
        compiler_params=pltpu.CompilerParams(
            dimension_semantics=("arbitrary", "arbitrary"), vmem_limit_bytes=VMEM_LIMIT),
        name="mixer",
    )(h, lw["norm_mix"], lw["w_main"], lw["w_if"], lw["b_if"], lw["conv_w"], lw["mh_gain"], lw["w_out"],
      lw["norm_ffn"], lw["w_route_t"], lw["b_route_t"], tri, *state)
    return outs[0], outs[1].reshape(seqs * seq_len), tuple(outs[2:])


def _expert_kernel(elo_ref, ehi_ref, nv_ref, nused_ref, inv_hbm, hx_hbm, nffn_ref, nfin_ref,
                   wg_lo, wg_hi, wu_lo, wu_hi, wd_lo, wd_hi, out_hbm,
                   idx_smem, xbuf, obuf, sem_i, sem_g, sem_s, *, tile, final_norm):
    j = pl.program_id(0)
    n_used = nused_ref[0]
    gather = RowGather(inv_hbm, hx_hbm, idx_smem, xbuf, sem_i, sem_g, tile)

    def scatter_wait(s):
        @pl.when(s >= 0)
        def _():
            rows = obuf.at[s % 2, pl.ds(0, nv_ref[jnp.maximum(s, 0)])]
            pltpu.make_async_copy(rows, rows, sem_s.at[s % 2]).wait()

    def step(phase):
        i_slot, o_slot = phase % IDX_SLOTS, phase % 2
        slot = gather.rows_ready(j, phase)
        x = xbuf[slot].reshape(tile, ROW_W)
        gather.prefetch(j, phase)
        hrow = x[:, 0:D_MODEL]
        w_lo = x[:, D_MODEL + 1:D_MODEL + 2]
        w_hi = x[:, D_MODEL + 2:D_MODEL + 3]
        xn = _rms(hrow, nffn_ref[...]).astype(_BF16)
        hid_lo = jax.nn.silu(_dot(xn, wg_lo[0])) * _dot(xn, wu_lo[0]) * w_lo
        hid_hi = jax.nn.silu(_dot(xn, wg_hi[0])) * _dot(xn, wu_hi[0]) * w_hi
        y = hrow + _dot(hid_lo.astype(_BF16), wd_lo[0]) + _dot(hid_hi.astype(_BF16), wd_hi[0])
        if final_norm:
            y = _rms(y, nfin_ref[...])
        scatter_wait(j - 2)
        obuf[o_slot, :, 0, :] = y

        nv = nv_ref[j]
        full = nv // SUBLANES

        def send(i):
            pltpu.make_async_copy(obuf.at[o_slot, i], out_hbm.at[pl.ds(idx_smem[i_slot, i], 1)],
                                  sem_s.at[o_slot]).start(priority=1)

        def send_group(g):
            for k in range(SUBLANES):
                send(g * SUBLANES + k)

        def send_row(i, carry):
            send(i)
            return carry
        for g in range(tile // SUBLANES):
            pl.when(g < full)(functools.partial(send_group, g))
        lax.fori_loop(full * SUBLANES, nv, send_row, 0)

    gather.first(j)

    @pl.when(j < n_used)
    def _():
        _by_phase(j, step)

    @pl.when(j == n_used)
    def _():
        gather.drain(j)
        scatter_wait(j - 2)
        scatter_wait(j - 1)


_PAIR_LO = np.array([0, 0, 0, 1, 1, 2], np.int32)
_PAIR_HI = np.array([1, 2, 3, 2, 3, 3], np.int32)


def _sort_plan(bucket, n_tokens, tile):
    n_tiles = -(-n_tokens // tile) + N_BUCKETS + 1
    ids = jnp.arange(N_BUCKETS, dtype=jnp.int32)
    order = jnp.argsort(bucket, stable=True).astype(jnp.int32)
    counts = jnp.sum((bucket[:, None] == ids[None, :]).astype(jnp.int32), axis=0)
    tiles = (counts + tile - 1) // tile
    tile_end = jnp.cumsum(tiles)
    tile_start = tile_end - tiles
    cnt_start = jnp.cumsum(counts) - counts
    n_used = tile_end[-1]
    j = jnp.arange(n_tiles + LOOKAHEAD_TILES, dtype=jnp.int32)
    jj = jnp.minimum(j, jnp.maximum(n_used - 1, 0))
    sel = (jj[:, None] >= tile_start[None, :]) & (jj[:, None] < tile_end[None, :])
    pick = lambda v: jnp.sum(jnp.where(sel, v[None, :], 0), axis=1)
    tb = pick(ids)
    first = pick(cnt_start) + (jj - pick(tile_start)) * tile
    nv = jnp.where(j < n_used, jnp.clip(pick(cnt_start + counts) - first, 0, tile), 0).astype(jnp.int32)
    k = jnp.arange(tile, dtype=jnp.int32)
    slot = jnp.clip(first[:, None] + k[None, :], 0, n_tokens - 1)
    spread = (j[:, None] * tile + k[None, :]) % n_tokens
    inv = jnp.where(k[None, :] < nv[:, None], order[slot], spread).reshape(-1).astype(jnp.int32)
    grp = tb // N_PAIRS
    pr = tb % N_PAIRS
    e_lo = (grp * EXP_PER_GROUP + jnp.asarray(_PAIR_LO)[pr]).astype(jnp.int32)
    e_hi = (grp * EXP_PER_GROUP + jnp.asarray(_PAIR_HI)[pr]).astype(jnp.int32)
    return e_lo, e_hi, nv, n_used.reshape(1).astype(jnp.int32), inv, n_tiles


def _experts(hx, bucket, lw, norm_final, final_norm):
    n_tokens = hx.shape[0]
    tile = min(EXPERT_ROWS, n_tokens)
    e_lo, e_hi, nv, n_used, inv, n_tiles = _sort_plan(bucket, n_tokens, tile)
    inv, idx_rows = _idx_tiles(inv, tile)
    any_spec = pl.BlockSpec(memory_space=pl.ANY)
    vec = pl.BlockSpec((1, D_MODEL), lambda j, lo, hi, nv, nu: (0, 0))
    w_in_lo = pl.BlockSpec((1, D_MODEL, D_EXPERT), lambda j, lo, hi, nv, nu: (lo[j], 0, 0))
    w_in_hi = pl.BlockSpec((1, D_MODEL, D_EXPERT), lambda j, lo, hi, nv, nu: (hi[j], 0, 0))
    w_dn_lo = pl.BlockSpec((1, D_EXPERT, D_MODEL), lambda j, lo, hi, nv, nu: (lo[j], 0, 0))
    w_dn_hi = pl.BlockSpec((1, D_EXPERT, D_MODEL), lambda j, lo, hi, nv, nu: (hi[j], 0, 0))
    return pl.pallas_call(
        functools.partial(_expert_kernel, tile=tile, final_norm=final_norm),
        grid_spec=pltpu.PrefetchScalarGridSpec(
            num_scalar_prefetch=4,
            grid=(n_tiles,),
            in_specs=[any_spec, any_spec, vec, vec, w_in_lo, w_in_hi, w_in_lo, w_in_hi, w_dn_lo, w_dn_hi],
            out_specs=any_spec,
            scratch_shapes=[
                pltpu.SMEM((IDX_SLOTS, idx_rows), jnp.int32),
                _row_buffer(tile, ROW_W),
                pltpu.VMEM((2, tile, 1, D_MODEL), _F32),
                pltpu.SemaphoreType.DMA((IDX_SLOTS,)),
                pltpu.SemaphoreType.DMA((2,)),
                pltpu.SemaphoreType.DMA((2,)),
            ],
        ),
        out_shape=jax.ShapeDtypeStruct((n_tokens, D_MODEL), _F32),
        compiler_params=pltpu.CompilerParams(
            dimension_semantics=("arbitrary",), vmem_limit_bytes=VMEM_LIMIT),
        name="experts",
    )(e_lo, e_hi, nv, n_used, inv, hx, lw["norm_ffn"], norm_final,
      lw["w_gate"], lw["w_gate"], lw["w_up"], lw["w_up"], lw["w_down"], lw["w_down"])


def _layer_weights(l, norm_mix, w_in, b_if, conv_w, mh_gain, w_out, norm_ffn,
                   w_group, b_group, w_router, b_router, w_gate, w_up, w_down):
    w = w_in[l].astype(_BF16)
    o_if = 2 * QK_W + V_W
    o_rest = o_if + 2 * M_HEADS
    w_main = jnp.concatenate([w[:, :o_if], w[:, o_rest:]], axis=1)
    w_if = jnp.pad(w[:, o_if:o_rest], ((0, 0), (0, LANES - 2 * M_HEADS)))
    pad_lanes = lambda a: jnp.pad(a, ((0, 0), (0, LANES - a.shape[1])))
    n_logits = N_GROUPS + N_GROUPS * EXP_PER_GROUP
    w_route = jnp.pad(jnp.concatenate([w_group[l], w_router[l]], axis=1).T, ((0, ROUTE_ROWS - n_logits), (0, 0)))
    w_route_hi = w_route.astype(_BF16)
    w_route_lo = (w_route - w_route_hi.astype(_F32)).astype(_BF16)
    b_route = jnp.pad(jnp.concatenate([b_group[l], b_router[l]]), (0, ROUTE_ROWS - n_logits))
    return {
        "norm_mix": norm_mix[l][None], "w_main": w_main, "w_if": w_if,
        "b_if": pad_lanes(b_if[l][None]), "conv_w": conv_w[l], "mh_gain": mh_gain[l][None],
        "w_out": w_out[l].astype(_BF16), "norm_ffn": norm_ffn[l][None],
        "w_route_t": jnp.concatenate([w_route_hi, w_route_lo], axis=0), "b_route_t": b_route[:, None],
        "w_gate": w_gate[l].astype(_BF16), "w_up": w_up[l].astype(_BF16), "w_down": w_down[l].astype(_BF16),
    }


def kernel(x, meta_tokens, norm_mix, w_in, b_if, conv_w, mh_gain, w_out, norm_ffn, w_group, b_group,
           w_router, b_router, w_gate, w_up, w_down, norm_final):
    batch, seq, d = x.shape
    depth = w_in.shape[0]
    assert d == D_MODEL and seq % MIX_ROWS == 0 and meta_tokens.shape == (N_META, D_MODEL)
    layers = [_layer_weights(l, norm_mix, w_in, b_if, conv_w, mh_gain, w_out, norm_ffn,
                             w_group, b_group, w_router, b_router, w_gate, w_up, w_down)
              for l in range(depth)]
    zero_state = (jnp.zeros((M_HEADS, M_DQK, M_DV), _F32), jnp.zeros((SUBLANES, M_DQK), _F32),
                  jnp.zeros((SUBLANES, LANES), _F32), jnp.zeros((SUBLANES, D_MODEL), _F32))

    hm = jnp.concatenate([jnp.zeros((REF_CHUNK - N_META, D_MODEL), _F32), meta_tokens.astype(_F32)], axis=0)
    nfin = norm_final[None]
    states = []
    for l in range(depth):
        hmx, info, st = _mixer(hm, 1, REF_CHUNK, layers[l], zero_state, emit_state=True)
        states.append(st)
        if l + 1 < depth:
            hm = _experts(hmx, info, layers[l], nfin, False)

    h = x.reshape(batch * seq, D_MODEL)
    for l in range(depth):
        hx, info, _ = _mixer(h, batch, seq, layers[l], states[l])
        h = _experts(hx, info, layers[l], nfin, l + 1 == depth)
    return h.reshape(batch, seq, D_MODEL)
```

```python
import functools

import jax
import jax.numpy as jnp
import numpy as np
from jax import lax
from jax.experimental import pallas as pl
from jax.experimental.pallas import tpu as pltpu

D_MODEL = 1024
N_META = 16
M_HEADS = 4
M_DQK = 128
M_DV = 256
REF_CHUNK = 64
N_GROUPS = 4
EXP_PER_GROUP = 4
N_PAIRS = 6
N_BUCKETS = N_GROUPS * N_PAIRS
D_EXPERT = 256
EPS = 1e-6
QK_W = M_HEADS * M_DQK
V_W = M_HEADS * M_DV

LANES = 128
SUBLANES = 8
ROW_W = D_MODEL + LANES
ROUTE_ROWS = 32
VMEM_LIMIT = 56 * 1024 * 1024

_OFF_Q = 0
_OFF_K = _OFF_Q + QK_W
_OFF_V = _OFF_K + QK_W
_OFF_GM = _OFF_V + V_W
_OFF_CX = _OFF_GM + V_W
_OFF_CB = _OFF_CX + D_MODEL
_OFF_CC = _OFF_CB + D_MODEL
_OFF_GC = _OFF_CC + D_MODEL
MAIN_W = _OFF_GC + D_MODEL

MIX_CHUNK = 256
MIX_ROWS = 512
EXPERT_ROWS = 256

_F32 = jnp.float32
_BF16 = jnp.bfloat16


def _rms(x, g):
    return x * lax.rsqrt(jnp.mean(x * x, axis=-1, keepdims=True) + EPS) * g


def _dot(a, b):
    return jnp.dot(a, b, preferred_element_type=_F32)


def _split2(x):
    hi = x.astype(_BF16)
    return hi, (x - hi.astype(_F32)).astype(_BF16)


MIN_IDX_TILE = 128


LOOKAHEAD_TILES = 2
IDX_SLOTS = 4


def _idx_tiles(idx, rows):
    idx_rows = max(rows, MIN_IDX_TILE)
    if idx_rows == rows:
        return idx, idx_rows
    return jnp.pad(idx.reshape(-1, rows), ((0, 0), (0, idx_rows - rows))).reshape(-1), idx_rows


def _row_buffer(rows, width):
    return pltpu.VMEM((2, rows // SUBLANES, SUBLANES, width), _F32)


PHASES = IDX_SLOTS


def _by_phase(step, fn):
    for phase in range(PHASES):
        pl.when(step % PHASES == phase)(functools.partial(fn, phase))


class RowGather:
    def __init__(self, idx_hbm, src_hbm, idx_smem, buf, sem_i, sem_g, rows):
        self.idx_hbm, self.src_hbm, self.idx_smem, self.buf = idx_hbm, src_hbm, idx_smem, buf
        self.sem_i, self.sem_g, self.rows = sem_i, sem_g, rows
        self.idx_rows = idx_smem.shape[1]

    def _idx_copy(self, s, slot):
        return pltpu.make_async_copy(self.idx_hbm.at[pl.ds(s * self.idx_rows, self.idx_rows)],
                                     self.idx_smem.at[slot], self.sem_i.at[slot])

    def _issue(self, phase):
        i_slot, b_slot = phase % IDX_SLOTS, phase % 2
        for i in range(self.rows):
            pltpu.make_async_copy(self.src_hbm.at[self.idx_smem[i_slot, i]],
                                  self.buf.at[b_slot, i // SUBLANES, pl.ds(i % SUBLANES, 1)],
                                  self.sem_g.at[b_slot]).start(priority=i % 2)

    def _wait_rows(self, slot):
        pltpu.make_async_copy(self.buf.at[slot], self.buf.at[slot], self.sem_g.at[slot]).wait()

    def first(self, step):
        @pl.when(step == 0)
        def _():
            self._idx_copy(step, 0).start()
            self._idx_copy(step, 0).wait()
            self._issue(0)
            self._idx_copy(step + 1, 1).start()

    def rows_ready(self, step, phase):
        self._idx_copy(step + 1, (phase + 1) % IDX_SLOTS).wait()
        self._wait_rows(phase % 2)
        return phase % 2

    def prefetch(self, step, phase):
        self._issue(phase + 1)
        self._idx_copy(step + 2, (phase + 2) % IDX_SLOTS).start()

    def drain(self, step):
        self._wait_rows(step % 2)
        self._idx_copy(step + 1, (step + 1) % IDX_SLOTS).wait()


def _route(lgt):
    g = [lgt[i:i + 1, :] for i in range(N_GROUPS)]
    gmax = jnp.maximum(jnp.maximum(g[0], g[1]), jnp.maximum(g[2], g[3]))
    gsum = sum(jnp.exp(gi - gmax) for gi in g)
    g_val = 1.0 / gsum
    g_idx = jnp.where(g[0] == gmax, 0, jnp.where(g[1] == gmax, 1, jnp.where(g[2] == gmax, 2, 3)))
    e = []
    for k in range(EXP_PER_GROUP):
        col = lambda i, k=k: lgt[N_GROUPS + EXP_PER_GROUP * i + k:N_GROUPS + EXP_PER_GROUP * i + k + 1, :]
        e.append(jnp.where(g_idx == 0, col(0), jnp.where(g_idx == 1, col(1), jnp.where(g_idx == 2, col(2), col(3)))))
    v1 = jnp.maximum(jnp.maximum(e[0], e[1]), jnp.maximum(e[2], e[3]))
    i1 = jnp.where(e[0] == v1, 0, jnp.where(e[1] == v1, 1, jnp.where(e[2] == v1, 2, 3)))
    neg = jnp.float32(-jnp.inf)
    r = [jnp.where(i1 == k, neg, e[k]) for k in range(EXP_PER_GROUP)]
    v2 = jnp.maximum(jnp.maximum(r[0], r[1]), jnp.maximum(r[2], r[3]))
    i2 = jnp.where((r[0] == v2) & (i1 != 0), 0,
                   jnp.where((r[1] == v2) & (i1 != 1), 1, jnp.where((r[2] == v2) & (i1 != 2), 2, 3)))
    ex = jnp.exp(v2 - v1)
    w1 = g_val / (1.0 + ex)
    w2 = g_val * ex / (1.0 + ex)
    lo = jnp.minimum(i1, i2)
    hi = jnp.maximum(i1, i2)
    w_lo = jnp.where(i1 < i2, w1, w2)
    w_hi = jnp.where(i1 < i2, w2, w1)
    pair = jnp.where(lo == 0, hi - 1, jnp.where(lo == 1, hi + 1, 5))
    bucket = (g_idx * N_PAIRS + pair).astype(_F32)
    row = lax.broadcasted_iota(jnp.int32, (SUBLANES, lgt.shape[1]), 0)
    return jnp.where(row == 0, bucket, jnp.where(row == 1, w_lo, jnp.where(row == 2, w_hi, 0.0)))


def _mixer_kernel(*refs, rows, chunk, emit_state):
    refs = list(refs)
    take = lambda n: [refs.pop(0) for _ in range(n)]
    (h_ref, nmix_ref, wmain_ref, wif_ref, bif_ref, convw_ref, gain_ref, wout_ref, nffn_ref, wrt_ref,
     brt_ref, tri_ref, c0_ref, n0_ref, m0_ref, u0_ref) = take(16)
    out_ref, bucket_ref = take(2)
    if emit_state:
        cN_ref, nN_ref, mN_ref, uN_ref = take(4)
    c_scr, n_scr, m_scr, ubuf, ybuf = take(5)
    t = pl.program_id(1)

    @pl.when(t == 0)
    def _():
        c_scr[...] = c0_ref[...]
        n_scr[...] = n0_ref[...]
        m_scr[...] = m0_ref[...]
        ubuf[0:SUBLANES, :] = u0_ref[...]

    h = h_ref[...]
    xn = _rms(h, nmix_ref[...]).astype(_BF16)

    def proj(off, width):
        return _dot(xn, wmain_ref[:, off:off + width])

    u = proj(_OFF_CC, D_MODEL) * proj(_OFF_CX, D_MODEL)
    ubuf[SUBLANES:SUBLANES + rows, :] = u
    conv = (convw_ref[0:1, :] * ubuf[SUBLANES - 2:SUBLANES - 2 + rows, :]
            + convw_ref[1:2, :] * ubuf[SUBLANES - 1:SUBLANES - 1 + rows, :]
            + convw_ref[2:3, :] * u)
    ubuf[0:SUBLANES, :] = ubuf[rows:rows + SUBLANES, :]
    ybuf[...] = jax.nn.sigmoid(proj(_OFF_GC, D_MODEL)) * (proj(_OFF_CB, D_MODEL) * conv)

    gates = _dot(xn, wif_ref[...]) + bif_ref[...]
    lane = lax.broadcasted_iota(jnp.int32, gates.shape, 1)
    is_f = (lane >= M_HEADS) & (lane < 2 * M_HEADS)
    logf = jnp.where(is_f, jax.nn.log_sigmoid(gates), 0.0)
    tri = tri_ref[...]
    f_hi = logf.astype(_BF16)
    f_r = logf - f_hi.astype(_F32)
    f_mid = f_r.astype(_BF16)
    f_lo = (f_r - f_mid.astype(_F32)).astype(_BF16)
    bcum = _dot(tri, f_hi) + _dot(tri, f_mid) + _dot(tri, f_lo)
    gates_t = gates.T
    bcum_t = bcum.T

    q_all = (proj(_OFF_Q, QK_W) * (M_DQK ** -0.5)).astype(_BF16)
    k_all = proj(_OFF_K, QK_W)
    v_all = proj(_OFF_V, V_W).astype(_BF16)
    gm_all = proj(_OFF_GM, V_W)

    ti = lax.broadcasted_iota(jnp.int32, (chunk, chunk), 0)
    si = lax.broadcasted_iota(jnp.int32, (chunk, chunk), 1)
    causal = si <= ti

    for c in range(rows // chunk):
        r0 = c * chunk
        for hd in range(M_HEADS):
            q = q_all[r0:r0 + chunk, hd * M_DQK:(hd + 1) * M_DQK]
            kf = k_all[r0:r0 + chunk, hd * M_DQK:(hd + 1) * M_DQK]
            v = v_all[r0:r0 + chunk, hd * M_DV:(hd + 1) * M_DV]
            b_c = bcum[r0:r0 + chunk, M_HEADS + hd:M_HEADS + hd + 1]
            i_c = gates[r0:r0 + chunk, hd:hd + 1]
            b_r = bcum_t[M_HEADS + hd:M_HEADS + hd + 1, r0:r0 + chunk]
            i_r = gates_t[hd:hd + 1, r0:r0 + chunk]
            m_prev = m_scr[hd:hd + 1, 0:1]
            c_prev = c_scr[hd]
            n_prev = n_scr[hd:hd + 1, :]

            log_d = jnp.where(causal, b_c - b_r + i_r, -jnp.inf)
            inter = b_c + m_prev
            m_t = jnp.maximum(inter, jnp.max(log_d, axis=-1, keepdims=True))
            dmat = jnp.exp(log_d - m_t)
            s = lax.dot_general(q, kf.astype(_BF16), (((1,), (1,)), ((), ())),
                                preferred_element_type=_F32) * dmat
            e_inter = jnp.exp(inter - m_t)
            num = _dot(s.astype(_BF16), v) + e_inter * _dot(q, c_prev.astype(_BF16))
            den = (jnp.sum(s, axis=-1, keepdims=True)
                   + e_inter * jnp.sum(q.astype(_F32) * n_prev, axis=-1, keepdims=True))
            hc = num / jnp.maximum(jnp.abs(den), jnp.exp(-m_t))

            b_last = b_c[chunk - 1:chunk, :]
            log_w = b_last - b_c + i_c
            m_new = jnp.maximum(b_last + m_prev, jnp.max(log_w, axis=0, keepdims=True))
            kw = kf * jnp.exp(log_w - m_new)
            decay = jnp.exp(b_last + m_prev - m_new)
            c_scr[hd] = decay * c_prev + lax.dot_general(
                kw.astype(_BF16), v, (((0,), (0,)), ((), ())), preferred_element_type=_F32)
            n_scr[hd:hd + 1, :] = decay * n_prev + jnp.sum(kw, axis=0, keepdims=True)
            m_scr[hd:hd + 1, :] = jnp.broadcast_to(m_new, (1, LANES))

            hn = hc * lax.rsqrt(jnp.mean(hc * hc, axis=-1, keepdims=True) + EPS)
            cols = slice(hd * M_DV, (hd + 1) * M_DV)
            ybuf[r0:r0 + chunk, cols] += (jax.nn.sigmoid(gm_all[r0:r0 + chunk, cols])
                                          * (hn * gain_ref[:, cols]))

    h1 = h + _dot(ybuf[...].astype(_BF16), wout_ref[...])

    x_hi, x_lo = _split2(_rms(h1, nffn_ref[...]))
    nt = (((1,), (1,)), ((), ()))
    hi_all = lax.dot_general(wrt_ref[...], x_hi, nt, preferred_element_type=_F32)
    lo_x = lax.dot_general(wrt_ref[0:ROUTE_ROWS, :], x_lo, nt, preferred_element_type=_F32)
    lgt = hi_all[0:ROUTE_ROWS] + hi_all[ROUTE_ROWS:2 * ROUTE_ROWS] + lo_x + brt_ref[...]
    info_t = _route(lgt)
    bucket_ref[0] = info_t[0:1, :].astype(jnp.int32)
    out_ref[:, 0, 0:D_MODEL] = h1
    out_ref[:, 0, D_MODEL:ROW_W] = jnp.concatenate(
        [info_t, jnp.zeros((LANES - SUBLANES, rows), _F32)], axis=0).T

    if emit_state:
        @pl.when(t == pl.num_programs(1) - 1)
        def _():
            cN_ref[...] = c_scr[...]
            nN_ref[...] = n_scr[...]
            mN_ref[...] = m_scr[...]
            uN_ref[...] = ubuf[0:SUBLANES, :]


def _mixer(h, seqs, seq_len, lw, state, emit_state=False):
    rows = min(MIX_ROWS, seq_len)
    chunk = min(MIX_CHUNK, rows)
    steps = seq_len // rows
    r = np.arange(rows)
    tri = jnp.asarray((r[:, None] // chunk == r[None, :] // chunk) & (r[None, :] <= r[:, None]), _BF16)
    const = lambda shape: pl.BlockSpec(shape, lambda b, t: (0,) * len(shape), pipeline_mode=pl.Buffered(1))
    state_shapes = [(M_HEADS, M_DQK, M_DV), (SUBLANES, M_DQK), (SUBLANES, LANES), (SUBLANES, D_MODEL)]
    in_specs = [
        pl.BlockSpec((rows, D_MODEL), lambda b, t: (b * steps + t, 0)),
        const((1, D_MODEL)), const((D_MODEL, MAIN_W)), const((D_MODEL, LANES)), const((1, LANES)),
        const((3, D_MODEL)), const((1, V_W)), const((D_MODEL, D_MODEL)), const((1, D_MODEL)),
        const((2 * ROUTE_ROWS, D_MODEL)), const((ROUTE_ROWS, 1)), const((rows, rows)),
    ] + [const(s) for s in state_shapes]
    out_specs = [pl.BlockSpec((rows, 1, ROW_W), lambda b, t: (b * steps + t, 0, 0)),
                 pl.BlockSpec((1, 1, rows), lambda b, t: (b * steps + t, 0, 0))]
    out_shape = [jax.ShapeDtypeStruct((seqs * seq_len, 1, ROW_W), _F32),
                 jax.ShapeDtypeStruct((seqs * steps, 1, rows), jnp.int32)]
    if emit_state:
        out_specs += [pl.BlockSpec(s, lambda b, t, n=len(s): (0,) * n) for s in state_shapes]
        out_shape += [jax.ShapeDtypeStruct(s, _F32) for s in state_shapes]
    scratch = [
        pltpu.VMEM((M_HEADS, M_DQK, M_DV), _F32),
        pltpu.VMEM((SUBLANES, M_DQK), _F32),
        pltpu.VMEM((SUBLANES, LANES), _F32),
        pltpu.VMEM((rows + SUBLANES, D_MODEL), _F32),
        pltpu.VMEM((rows, D_MODEL), _F32),
    ]
    outs = pl.pallas_call(
        functools.partial(_mixer_kernel, rows=rows, chunk=chunk, emit_state=emit_state),
        grid=(seqs, steps),
        in_specs=in_specs,
        out_specs=out_specs,
        out_shape=out_shape,
        scratch_shapes=scratch,
        compiler_params=pltpu.CompilerParams(
            dimension_semantics=("arbitrary", "arbitrary"), vmem_limit_bytes=VMEM_LIMIT),
        name="mixer",
    )(h, lw["norm_mix"], lw["w_main"], lw["w_if"], lw["b_if"], lw["conv_w"], lw["mh_gain"], lw["w_out"],
      lw["norm_ffn"], lw["w_route_t"], lw["b_route_t"], tri, *state)
    return outs[0], outs[1].reshape(seqs * seq_len), tuple(outs[2:])


def _expert_kernel(elo_ref, ehi_ref, nv_ref, nused_ref, inv_hbm, hx_hbm, nffn_ref, nfin_ref,
                   wg_lo, wg_hi, wu_lo, wu_hi, wd_lo, wd_hi, out_hbm,
                   idx_smem, xbuf, obuf, sem_i, sem_g, sem_s, *, tile, final_norm):
    j = pl.program_id(0)
    n_used = nused_ref[0]
    gather = RowGather(inv_hbm, hx_hbm, idx_smem, xbuf, sem_i, sem_g, tile)

    def scatter_wait(s):
        @pl.when(s >= 0)
        def _():
            rows = obuf.at[s % 2, pl.ds(0, nv_ref[jnp.maximum(s, 0)])]
            pltpu.make_async_copy(rows, rows, sem_s.at[s % 2]).wait()

    def step(phase):
        i_slot, o_slot = phase % IDX_SLOTS, phase % 2
        slot = gather.rows_ready(j, phase)
        x = xbuf[slot].reshape(tile, ROW_W)
        gather.prefetch(j, phase)
        hrow = x[:, 0:D_MODEL]
        w_lo = x[:, D_MODEL + 1:D_MODEL + 2]
        w_hi = x[:, D_MODEL + 2:D_MODEL + 3]
        xn = _rms(hrow, nffn_ref[...]).astype(_BF16)
        hid_lo = jax.nn.silu(_dot(xn, wg_lo[0])) * _dot(xn, wu_lo[0]) * w_lo
        hid_hi = jax.nn.silu(_dot(xn, wg_hi[0])) * _dot(xn, wu_hi[0]) * w_hi
        y = hrow + _dot(hid_lo.astype(_BF16), wd_lo[0]) + _dot(hid_hi.astype(_BF16), wd_hi[0])
        if final_norm:
            y = _rms(y, nfin_ref[...])
        scatter_wait(j - 2)
        obuf[o_slot, :, 0, :] = y

        nv = nv_ref[j]
        full = nv // SUBLANES

        def send(i, priority):
            pltpu.make_async_copy(obuf.at[o_slot, i], out_hbm.at[pl.ds(idx_smem[i_slot, i], 1)],
                                  sem_s.at[o_slot]).start(priority=priority)

        def send_group(g):
            for k in range(SUBLANES):
                send(g * SUBLANES + k, k % 2)

        def send_row(i, carry):
            send(i, 0)
            return carry
        for g in range(tile // SUBLANES):
            pl.when(g < full)(functools.partial(send_group, g))
        lax.fori_loop(full * SUBLANES, nv, send_row, 0)

    gather.first(j)

    @pl.when(j < n_used)
    def _():
        _by_phase(j, step)

    @pl.when(j == n_used)
    def _():
        gather.drain(j)
        scatter_wait(j - 2)
        scatter_wait(j - 1)


_PAIR_LO = np.array([0, 0, 0, 1, 1, 2], np.int32)
_PAIR_HI = np.array([1, 2, 3, 2, 3, 3], np.int32)


def _sort_plan(bucket, n_tokens, tile):
    n_tiles = -(-n_tokens // tile) + N_BUCKETS + 1
    ids = jnp.arange(N_BUCKETS, dtype=jnp.int32)
    order = jnp.argsort(bucket, stable=True).astype(jnp.int32)
    counts = jnp.sum((bucket[:, None] == ids[None, :]).astype(jnp.int32), axis=0)
    tiles = (counts + tile - 1) // tile
    tile_end = jnp.cumsum(tiles)
    tile_start = tile_end - tiles
    cnt_start = jnp.cumsum(counts) - counts
    n_used = tile_end[-1]
    j = jnp.arange(n_tiles + LOOKAHEAD_TILES, dtype=jnp.int32)
    jj = jnp.minimum(j, jnp.maximum(n_used - 1, 0))
    sel = (jj[:, None] >= tile_start[None, :]) & (jj[:, None] < tile_end[None, :])
    pick = lambda v: jnp.sum(jnp.where(sel, v[None, :], 0), axis=1)
    tb = pick(ids)
    first = pick(cnt_start) + (jj - pick(tile_start)) * tile
    nv = jnp.where(j < n_used, jnp.clip(pick(cnt_start + counts) - first, 0, tile), 0).astype(jnp.int32)
    k = jnp.arange(tile, dtype=jnp.int32)
    slot = jnp.clip(first[:, None] + k[None, :], 0, n_tokens - 1)
    spread = (j[:, None] * tile + k[None, :]) % n_tokens
    inv = jnp.where(k[None, :] < nv[:, None], order[slot], spread).reshape(-1).astype(jnp.int32)
    grp = tb // N_PAIRS
    pr = tb % N_PAIRS
    e_lo = (grp * EXP_PER_GROUP + jnp.asarray(_PAIR_LO)[pr]).astype(jnp.int32)
    e_hi = (grp * EXP_PER_GROUP + jnp.asarray(_PAIR_HI)[pr]).astype(jnp.int32)
    return e_lo, e_hi, nv, n_used.reshape(1).astype(jnp.int32), inv, n_tiles


def _experts(hx, bucket, lw, norm_final, final_norm):
    n_tokens = hx.shape[0]
    tile = min(EXPERT_ROWS, n_tokens)
    e_lo, e_hi, nv, n_used, inv, n_tiles = _sort_plan(bucket, n_tokens, tile)
    inv, idx_rows = _idx_tiles(inv, tile)
    any_spec = pl.BlockSpec(memory_space=pl.ANY)
    vec = pl.BlockSpec((1, D_MODEL), lambda j, lo, hi, nv, nu: (0, 0))
    w_in_lo = pl.BlockSpec((1, D_MODEL, D_EXPERT), lambda j, lo, hi, nv, nu: (lo[j], 0, 0))
    w_in_hi = pl.BlockSpec((1, D_MODEL, D_EXPERT), lambda j, lo, hi, nv, nu: (hi[j], 0, 0))
    w_dn_lo = pl.BlockSpec((1, D_EXPERT, D_MODEL), lambda j, lo, hi, nv, nu: (lo[j], 0, 0))
    w_dn_hi = pl.BlockSpec((1, D_EXPERT, D_MODEL), lambda j, lo, hi, nv, nu: (hi[j], 0, 0))
    return pl.pallas_call(
        functools.partial(_expert_kernel, tile=tile, final_norm=final_norm),
        grid_spec=pltpu.PrefetchScalarGridSpec(
            num_scalar_prefetch=4,
            grid=(n_tiles,),
            in_specs=[any_spec, any_spec, vec, vec, w_in_lo, w_in_hi, w_in_lo, w_in_hi, w_dn_lo, w_dn_hi],
            out_specs=any_spec,
            scratch_shapes=[
                pltpu.SMEM((IDX_SLOTS, idx_rows), jnp.int32),
                _row_buffer(tile, ROW_W),
                pltpu.VMEM((2, tile, 1, D_MODEL), _F32),
                pltpu.SemaphoreType.DMA((IDX_SLOTS,)),
                pltpu.SemaphoreType.DMA((2,)),
                pltpu.SemaphoreType.DMA((2,)),
            ],
        ),
        out_shape=jax.ShapeDtypeStruct((n_tokens, D_MODEL), _F32),
        compiler_params=pltpu.CompilerParams(
            dimension_semantics=("arbitrary",), vmem_limit_bytes=VMEM_LIMIT),
        name="experts",
    )(e_lo, e_hi, nv, n_used, inv, hx, lw["norm_ffn"], norm_final,
      lw["w_gate"], lw["w_gate"], lw["w_up"], lw["w_up"], lw["w_down"], lw["w_down"])


def _layer_weights(l, norm_mix, w_in, b_if, conv_w, mh_gain, w_out, norm_ffn,
                   w_group, b_group, w_router, b_router, w_gate, w_up, w_down):
    w = w_in[l]
    o_if = 2 * QK_W + V_W
    o_rest = o_if + 2 * M_HEADS
    w_main = jnp.concatenate([w[:, :o_if], w[:, o_rest:]], axis=1).astype(_BF16)
    w_if = jnp.pad(w[:, o_if:o_rest], ((0, 0), (0, LANES - 2 * M_HEADS))).astype(_BF16)
    pad_lanes = lambda a: jnp.pad(a, ((0, 0), (0, LANES - a.shape[1])))
    n_logits = N_GROUPS + N_GROUPS * EXP_PER_GROUP
    w_route = jnp.pad(jnp.concatenate([w_group[l], w_router[l]], axis=1).T, ((0, ROUTE_ROWS - n_logits), (0, 0)))
    w_route_hi = w_route.astype(_BF16)
    w_route_lo = (w_route - w_route_hi.astype(_F32)).astype(_BF16)
    b_route = jnp.pad(jnp.concatenate([b_group[l], b_router[l]]), (0, ROUTE_ROWS - n_logits))
    return {
        "norm_mix": norm_mix[l][None], "w_main": w_main, "w_if": w_if,
        "b_if": pad_lanes(b_if[l][None]), "conv_w": conv_w[l], "mh_gain": mh_gain[l][None],
        "w_out": w_out[l].astype(_BF16), "norm_ffn": norm_ffn[l][None],
        "w_route_t": jnp.concatenate([w_route_hi, w_route_lo], axis=0), "b_route_t": b_route[:, None],
        "w_gate": w_gate[l].astype(_BF16), "w_up": w_up[l].astype(_BF16), "w_down": w_down[l].astype(_BF16),
    }


def kernel(x, meta_tokens, norm_mix, w_in, b_if, conv_w, mh_gain, w_out, norm_ffn, w_group, b_group,
           w_router, b_router, w_gate, w_up, w_down, norm_final):
    batch, seq, d = x.shape
    depth = w_in.shape[0]
    assert d == D_MODEL and seq % MIX_ROWS == 0 and meta_tokens.shape == (N_META, D_MODEL)
    layers = [_layer_weights(l, norm_mix, w_in, b_if, conv_w, mh_gain, w_out, norm_ffn,
                             w_group, b_group, w_router, b_router, w_gate, w_up, w_down)
              for l in range(depth)]
    zero_state = (jnp.zeros((M_HEADS, M_DQK, M_DV), _F32), jnp.zeros((SUBLANES, M_DQK), _F32),
                  jnp.zeros((SUBLANES, LANES), _F32), jnp.zeros((SUBLANES, D_MODEL), _F32))

    hm = jnp.concatenate([jnp.zeros((REF_CHUNK - N_META, D_MODEL), _F32), meta_tokens.astype(_F32)], axis=0)
    nfin = norm_final[None]
    states = []
    for l in range(depth):
        hmx, info, st = _mixer(hm, 1, REF_CHUNK, layers[l], zero_state, emit_state=True)
        states.append(st)
        if l + 1 < depth:
            hm = _experts(hmx, info, layers[l], nfin, False)

    h = x.reshape(batch * seq, D_MODEL)
    for l in range(depth):
        hx, info, _ = _mixer(h, batch, seq, layers[l], states[l])
        h = _experts(hx, info, layers[l], nfin, l + 1 == depth)
    return h.reshape(batch, seq, D_MODEL)
```
